```python
import jax, jax.numpy as jnp
from jax import lax
import numpy as np

D_MODEL = 4096
BATCH = 2
SEQ = 8192
DEPTH = 4

GRID_W = 64
CTX_LEN = 256
N_MIXERS = 3
N_LAYERS_A = (DEPTH + 2) // 3
N_LAYERS_B = (DEPTH + 1) // 3
N_LAYERS_C = DEPTH // 3
ADA_RANK = 256
N_MOD = 9
D_FF = 5 * D_MODEL // 4
MACARON_WEIGHT = 0.5
DEEPNORM_ALPHA = (2.0 * DEPTH) ** 0.25
DEEPNORM_BETA = (8.0 * DEPTH) ** -0.25
LN_EPS = 1e-5
RMS_EPS = 1e-6
CHUNK = 64
H_A = 8
DK_A = D_MODEL // (2 * H_A)
DV_A = D_MODEL // H_A
QK_A = H_A * DK_A
GATE_RANK_A = 16
GATE_NORMALIZER_A = 16.0
H_B = 8
DK_B = D_MODEL // (2 * H_B)
DV_B = D_MODEL // H_B
QK_B = H_B * DK_B
GATE_SOFTCAP = 15.0
M_INIT = -1e30
H_C = 32
Q_RANK = D_MODEL // 4
KV_RANK = D_MODEL // 8
D_NOPE = 128
D_ROPE = 64
D_V_C = 128
MLA_SCALE = (D_NOPE + D_ROPE) ** -0.5
ROPE_THETA = 10000.0
Q_BLOCK = 128

kernel_name = "hybrid_gla_mlstm_mla_macaron_deepnorm_dit"


def layer_norm(x, g, b):
    xf = x.astype(jnp.float32)
    mu = jnp.mean(xf, axis=-1, keepdims=True)
    var = jnp.mean(jnp.square(xf - mu), axis=-1, keepdims=True)
    return ((xf - mu) * lax.rsqrt(var + LN_EPS) * g + b).astype(x.dtype)


def rms_norm(x, g):
    xf = x.astype(jnp.float32)
    return (xf * lax.rsqrt(jnp.mean(xf * xf, axis=-1, keepdims=True) + RMS_EPS) * g).astype(x.dtype)


def head_rms_norm(o, g, n_heads):
    b_, l_, _ = o.shape
    of = o.astype(jnp.float32).reshape(b_, l_, n_heads, -1)
    of = of * lax.rsqrt(jnp.mean(of * of, axis=-1, keepdims=True) + RMS_EPS)
    return of.reshape(b_, l_, -1) * g


def to_heads(t, n_heads):
    return t.reshape(t.shape[0], t.shape[1], n_heads, -1).transpose(0, 2, 1, 3)


def from_heads(t):
    return t.transpose(0, 2, 1, 3).reshape(t.shape[0], t.shape[2], -1)


def flip_seq(t):
    return jnp.flip(t, axis=2)


def soft_cap(x):
    return GATE_SOFTCAP * jnp.tanh(x / GATE_SOFTCAP)


def modulation(silu_cond, w_down, w_up, bias):
    m = (silu_cond @ w_down) @ w_up + bias
    return m.reshape(silu_cond.shape[0], N_MOD, 1, D_MODEL)


def modulate(x, m, s):
    return x * (1.0 + m[:, 3 * s + 1]) + m[:, 3 * s]


def residual(x, y, m, s, weight, g, b):
    return layer_norm(DEEPNORM_ALPHA * x + weight * m[:, 3 * s + 2] * y, g, b)


def swiglu(h, w_in, w_out):
    a, u = jnp.split(h @ w_in, 2, axis=-1)
    return (jax.nn.silu(a) * u) @ w_out


def ffn_sublayer(x, m, s, w_in, w_out, g, b):
    return residual(x, swiglu(modulate(x, m, s), w_in, w_out), m, s, MACARON_WEIGHT, g, b)


def axial_rope(length):
    n_rows = length // GRID_W
    rows = jnp.repeat(jnp.arange(n_rows, dtype=jnp.float32), GRID_W)
    cols = jnp.tile(jnp.arange(GRID_W, dtype=jnp.float32), n_rows)
    half = D_ROPE // 2
    inv_freq = ROPE_THETA ** (-jnp.arange(0, half, 2, dtype=jnp.float32) / half)
    ang = jnp.concatenate([rows[:, None] * inv_freq, cols[:, None] * inv_freq], axis=-1)
    return jnp.cos(ang), jnp.sin(ang)


def apply_rope(x, cos, sin):
    x1, x2 = jnp.split(x, 2, axis=-1)
    cos = cos[:, None, :].astype(x.dtype)
    sin = sin[:, None, :].astype(x.dtype)
    return jnp.concatenate([x1 * cos - x2 * sin, x2 * cos + x1 * sin], axis=-1)


def gla_state_update(s, k, v, b):
    b_last = b[:, :, -1]
    return jnp.exp(b_last)[..., None] * s + jnp.einsum('bhjd,bhje->bhde', k * jnp.exp(b_last[:, :, None] - b), v)


def gla_scan(q, k, v, log_a, s0):
    b_, h_, l_, _ = q.shape
    n = l_ // CHUNK
    lower = jnp.tril(jnp.ones((CHUNK, CHUNK), dtype=bool))[:, :, None]

    def chunks(t):
        return jnp.moveaxis(t.reshape(b_, h_, n, CHUNK, t.shape[-1]), 2, 0)

    def step(s, blk):
        qc, kc, vc, ac = blk
        b = jnp.cumsum(ac, axis=2)
        rel = jnp.where(lower, b[:, :, :, None, :] - b[:, :, None, :, :], -jnp.inf)
        scores = jnp.einsum('bhid,bhjd,bhijd->bhij', qc, kc, jnp.exp(rel))
        o = jnp.einsum('bhij,bhje->bhie', scores, vc) + jnp.einsum('bhid,bhde->bhie', qc * jnp.exp(b), s)
        return gla_state_update(s, kc, vc, b), o

    s_fin, o = lax.scan(step, s0, tuple(chunks(t) for t in (q, k, v, log_a)))
    return jnp.moveaxis(o, 0, 2).reshape(b_, h_, l_, -1), s_fin


def gla_mixer(h_lat, h_ctx, w_in, w_gdown, w_gup, b_g, norm_g, w_out, need_ctx_out):
    def project(h, full):
        if full:
            q, k, v, g = jnp.split(h @ w_in, [QK_A, 2 * QK_A, 2 * QK_A + D_MODEL], axis=-1)
            q = to_heads(q * DK_A ** -0.5, H_A)
        else:
            k, v = jnp.split(h @ w_in[:, QK_A:2 * QK_A + D_MODEL], [QK_A], axis=-1)
            q = g = None
        log_a = tuple(
            to_heads(jax.nn.log_sigmoid(((h @ w_gdown[d]) @ w_gup[d] + b_g[d]).astype(jnp.float32))
                     / GATE_NORMALIZER_A, H_A)
            for d in range(2))
        return q, to_heads(k, H_A), to_heads(v, H_A), g, log_a

    def readout(o, g):
        o = head_rms_norm(from_heads(o), norm_g, H_A) * jax.nn.silu(g.astype(jnp.float32))
        return o.astype(g.dtype) @ w_out

    s0 = jnp.zeros((h_ctx.shape[0], H_A, DK_A, DV_A), jnp.float32)
    qc, kc, vc, gc, (ac_f, ac_b) = project(h_ctx, need_ctx_out)
    if need_ctx_out:
        oc_f, sc_f = gla_scan(qc, kc, vc, ac_f, s0)
        oc_b, sc_b = gla_scan(flip_seq(qc), flip_seq(kc), flip_seq(vc), flip_seq(ac_b), s0)
        y_ctx = readout(oc_f + flip_seq(oc_b), gc)
    else:
        sc_f = gla_state_update(s0, kc, vc, jnp.cumsum(ac_f, axis=2))
        sc_b = gla_state_update(s0, flip_seq(kc), flip_seq(vc), jnp.cumsum(flip_seq(ac_b), axis=2))
        y_ctx = None
    q, k, v, g, (a_f, a_b) = project(h_lat, True)
    o_f, _ = gla_scan(q, k, v, a_f, sc_f)
    o_b, _ = gla_scan(flip_seq(q), flip_seq(k), flip_seq(v), flip_seq(a_b), sc_b)
    return readout(o_f + flip_seq(o_b), g), y_ctx


def mlstm_state_update(state, k, v, log_i, b):
    c_mat, n_vec, m = state
    b_last = b[..., -1]
    w_log = b_last[..., None] - b + log_i
    m_new = jnp.maximum(b_last + m, jnp.max(w_log, axis=-1))
    keep = jnp.exp(b_last + m - m_new)
    w = jnp.exp(w_log - m_new[..., None])
    c_new = keep[..., None, None] * c_mat + jnp.einsum('bhj,bhjd,bhje->bhde', w, k, v)
    n_new = keep[..., None] * n_vec + jnp.einsum('bhj,bhjd->bhd', w, k)
    return (c_new, n_new, m_new)


def mlstm_scan(q, k, v, log_i, log_f, state):
    b_, h_, l_, _ = q.shape
    n = l_ // CHUNK
    lower = jnp.tril(jnp.ones((CHUNK, CHUNK), dtype=bool))

    def chunks(t):
        return jnp.moveaxis(t.reshape((b_, h_, n, CHUNK) + t.shape[3:]), 2, 0)

    def step(st, blk):
        qc, kc, vc, ic, fc = blk
        c_mat, n_vec, m = st
        b = jnp.cumsum(fc, axis=-1)
        d_log = jnp.where(lower, b[..., :, None] - b[..., None, :] + ic[..., None, :], -jnp.inf)
        inter_log = b + m[..., None]
        m_row = jnp.maximum(inter_log, jnp.max(d_log, axis=-1))
        w_intra = jnp.exp(d_log - m_row[..., None])
        w_inter = jnp.exp(inter_log - m_row)
        s = jnp.einsum('bhid,bhjd->bhij', qc, kc) * w_intra
        num = jnp.einsum('bhij,bhje->bhie', s, vc) + w_inter[..., None] * jnp.einsum('bhid,bhde->bhie', qc, c_mat)
        den = jnp.sum(s, axis=-1) + w_inter * jnp.einsum('bhid,bhd->bhi', qc, n_vec)
        h = num / jnp.maximum(jnp.abs(den), jnp.exp(-m_row))[..., None]
        return mlstm_state_update(st, kc, vc, ic, b), h

    st_fin, h = lax.scan(step, state, tuple(chunks(t) for t in (q, k, v, log_i, log_f)))
    return jnp.moveaxis(h, 0, 2).reshape(b_, h_, l_, -1), st_fin


def mlstm_mixer(h_lat, h_ctx, w_in, w_if, b_if, norm_g, w_out, need_ctx_out):
    def project(h, full):
        if full:
            q, k, v, o = jnp.split(h @ w_in, [QK_B, 2 * QK_B, 2 * QK_B + D_MODEL], axis=-1)
            q = to_heads(q * DK_B ** -0.5, H_B)
        else:
            k, v = jnp.split(h @ w_in[:, QK_B:2 * QK_B + D_MODEL], [QK_B], axis=-1)
            q = o = None
        gates = []
        for d in range(2):
            pre = soft_cap((h @ w_if[d] + b_if[d]).astype(jnp.float32)).transpose(0, 2, 1)
            gates.append((pre[:, :H_B], jax.nn.log_sigmoid(pre[:, H_B:])))
        return q, to_heads(k, H_B), to_heads(v, H_B), o, gates

    def readout(h_sum, o):
        hn = head_rms_norm(from_heads(h_sum), norm_g, H_B) * jax.nn.sigmoid(o.astype(jnp.float32))
        return hn.astype(o.dtype) @ w_out

    b_ = h_ctx.shape[0]
    st0 = (jnp.zeros((b_, H_B, DK_B, DV_B), jnp.float32),
           jnp.zeros((b_, H_B, DK_B), jnp.float32),
           jnp.full((b_, H_B), M_INIT, jnp.float32))
    qc, kc, vc, oc, ((ic_f, fc_f), (ic_b, fc_b)) = project(h_ctx, need_ctx_out)
    if need_ctx_out:
        hc_f, st_f = mlstm_scan(qc, kc, vc, ic_f, fc_f, st0)
        hc_b, st_b = mlstm_scan(flip_seq(qc), flip_seq(kc), flip_seq(vc), flip_seq(ic_b), flip_seq(fc_b), st0)
        y_ctx = readout(hc_f + flip_seq(hc_b), oc)
    else:
        st_f = mlstm_state_update(st0, kc, vc, ic_f, jnp.cumsum(fc_f, axis=-1))
        st_b = mlstm_state_update(st0, flip_seq(kc), flip_seq(vc), flip_seq(ic_b),
                                  jnp.cumsum(flip_seq(fc_b), axis=-1))
        y_ctx = None
    q, k, v, o, ((i_f, f_f), (i_b, f_b)) = project(h_lat, True)
    h_f, _ = mlstm_scan(q, k, v, i_f, f_f, st_f)
    h_b, _ = mlstm_scan(flip_seq(q), flip_seq(k), flip_seq(v), flip_seq(i_b), flip_seq(f_b), st_b)
    return readout(h_f + flip_seq(h_b), o), y_ctx


def mla_q(cq, q_norm_g, w_uq, rope):
    b_, l_, _ = cq.shape
    q = (rms_norm(cq, q_norm_g) @ w_uq).reshape(b_, l_, H_C, D_NOPE + D_ROPE)
    q_nope, q_rope = q[..., :D_NOPE], q[..., D_NOPE:]
    if rope is not None:
        q_rope = apply_rope(q_rope, *rope)
    return q_nope, q_rope


def mla_kv(ckv_kr, kv_norm_g, w_ukv, rope):
    b_, l_, _ = ckv_kr.shape
    ckv, k_rope = ckv_kr[..., :KV_RANK], ckv_kr[..., KV_RANK:]
    kv = (rms_norm(ckv, kv_norm_g) @ w_ukv).reshape(b_, l_, H_C, D_NOPE + D_V_C)
    k_nope, v = kv[..., :D_NOPE], kv[..., D_NOPE:]
    if rope is not None:
        k_rope = apply_rope(k_rope[:, :, None, :], *rope)[:, :, 0]
    return k_nope, k_rope, v


def mla_attend(q_nope, q_rope, k_nope, k_rope, v):
    s = jnp.einsum('bqhd,bkhd->bhqk', q_nope, k_nope) + jnp.einsum('bqhr,bkr->bhqk', q_rope, k_rope)
    p = jax.nn.softmax(s.astype(jnp.float32) * MLA_SCALE, axis=-1).astype(v.dtype)
    o = jnp.einsum('bhqk,bkhd->bqhd', p, v)
    return o.reshape(o.shape[0], o.shape[1], -1)


def mla_mixer(h_lat, h_ctx, w_down, q_norm_g, w_uq, kv_norm_g, w_ukv, w_out, need_ctx_out):
    b_, l_, _ = h_lat.shape
    rope = axial_rope(l_)
    z = h_lat @ w_down
    q_nope, q_rope = mla_q(z[..., :Q_RANK], q_norm_g, w_uq, rope)
    k_nope, k_rope, v = mla_kv(z[..., Q_RANK:], kv_norm_g, w_ukv, rope)
    if need_ctx_out:
        zc = h_ctx @ w_down
        cq_c, ckv_c = zc[..., :Q_RANK], zc[..., Q_RANK:]
    else:
        ckv_c = h_ctx @ w_down[:, Q_RANK:]
    k_nope_c, k_rope_c, v_c = mla_kv(ckv_c, kv_norm_g, w_ukv, None)
    k_nope_all = jnp.concatenate([k_nope, k_nope_c], axis=1)
    k_rope_all = jnp.concatenate([k_rope, k_rope_c], axis=1)
    v_all = jnp.concatenate([v, v_c], axis=1)
    nb = l_ // Q_BLOCK

    def blocks(t):
        return jnp.moveaxis(t.reshape((b_, nb, Q_BLOCK) + t.shape[2:]), 1, 0)

    o = lax.map(lambda qb: mla_attend(qb[0], qb[1], k_nope_all, k_rope_all, v_all),
                (blocks(q_nope), blocks(q_rope)))
    y = jnp.moveaxis(o, 0, 1).reshape(b_, l_, H_C * D_V_C) @ w_out
    y_ctx = None
    if need_ctx_out:
        q_nope_c, q_rope_c = mla_q(cq_c, q_norm_g, w_uq, None)
        y_ctx = mla_attend(q_nope_c, q_rope_c, k_nope_c, k_rope_c, v_c) @ w_out
    return y, y_ctx


def setup_inputs(seed: int = 0) -> dict:
    key = jax.random.key(seed)
    ks = iter(jax.random.split(key, 32))
    D = D_MODEL

    def nrm(shape, scale):
        return jax.random.normal(next(ks), shape, jnp.float32) * scale

    return {
        "x": nrm((BATCH, SEQ, D), 1.0),
        "c": nrm((BATCH, D), 1.0),
        "ctx": nrm((BATCH, CTX_LEN, D), 1.0),
        "c_ctx": nrm((D,), 1.0),
        "ada_down": nrm((DEPTH, D, ADA_RANK), D ** -0.5),
        "ada_up": nrm((DEPTH, ADA_RANK, N_MOD * D), 0.5 * ADA_RANK ** -0.5),
        "ada_bias": nrm((DEPTH, N_MOD * D), 0.02),
        "ln_g": 1.0 + nrm((DEPTH, 3, D), 0.02),
        "ln_b": nrm((DEPTH, 3, D), 0.02),
        "ffn_w_in": nrm((DEPTH, 2, D, 2 * D_FF), D ** -0.5),
        "ffn_w_out": nrm((DEPTH, 2, D_FF, D), DEEPNORM_BETA * D_FF ** -0.5),
        "gla_w_in": nrm((N_LAYERS_A, D, 2 * QK_A + 2 * D), D ** -0.5),
        "gla_w_gdown": nrm((N_LAYERS_A, 2, D, GATE_RANK_A), D ** -0.5),
        "gla_w_gup": nrm((N_LAYERS_A, 2, GATE_RANK_A, QK_A), GATE_RANK_A ** -0.5),
        "gla_b_g": nrm((N_LAYERS_A, 2, QK_A), 0.1),
        "gla_norm_g": 1.0 + nrm((N_LAYERS_A, D), 0.02),
        "gla_w_out": nrm((N_LAYERS_A, D, D), DEEPNORM_BETA * D ** -0.5),
        "mlstm_w_in": nrm((N_LAYERS_B, D, 2 * QK_B + 2 * D), D ** -0.5),
        "mlstm_w_if": nrm((N_LAYERS_B, 2, D, 2 * H_B), D ** -0.5),
        "mlstm_b_if": jnp.concatenate([nrm((N_LAYERS_B, 2, H_B), 0.1),
                                        3.0 + nrm((N_LAYERS_B, 2, H_B), 0.5)], axis=-1),
        "mlstm_norm_g": 1.0 + nrm((N_LAYERS_B, D), 0.02),
        "mlstm_w_out": nrm((N_LAYERS_B, D, D), DEEPNORM_BETA * D ** -0.5),
        "mla_w_down": nrm((N_LAYERS_C, D, Q_RANK + KV_RANK + D_ROPE), D ** -0.5),
        "mla_q_norm_g": 1.0 + nrm((N_LAYERS_C, Q_RANK), 0.02),
        "mla_w_uq": nrm((N_LAYERS_C, Q_RANK, H_C * (D_NOPE + D_ROPE)), Q_RANK ** -0.5),
        "mla_kv_norm_g": 1.0 + nrm((N_LAYERS_C, KV_RANK), 0.02),
        "mla_w_ukv": nrm((N_LAYERS_C, KV_RANK, H_C * (D_NOPE + D_V_C)), KV_RANK ** -0.5),
        "mla_w_out": nrm((N_LAYERS_C, H_C * D_V_C, D), DEEPNORM_BETA * (H_C * D_V_C) ** -0.5),
    }


def reference(x, c, ctx, c_ctx, ada_down, ada_up, ada_bias, ln_g, ln_b, ffn_w_in, ffn_w_out,
              gla_w_in, gla_w_gdown, gla_w_gup, gla_b_g, gla_norm_g, gla_w_out,
              mlstm_w_in, mlstm_w_if, mlstm_b_if, mlstm_norm_g, mlstm_w_out,
              mla_w_down, mla_q_norm_g, mla_w_uq, mla_kv_norm_g, mla_w_ukv, mla_w_out):
    silu_c = jax.nn.silu(c)
    silu_c_ctx = jax.nn.silu(c_ctx)[None]
    xc = ctx
    for i in range(DEPTH):
        last = i == DEPTH - 1
        m = modulation(silu_c, ada_down[i], ada_up[i], ada_bias[i])
        mc = modulation(silu_c_ctx, ada_down[i], ada_up[i], ada_bias[i])
        x = ffn_sublayer(x, m, 0, ffn_w_in[i, 0], ffn_w_out[i, 0], ln_g[i, 0], ln_b[i, 0])
        xc = ffn_sublayer(xc, mc, 0, ffn_w_in[i, 0], ffn_w_out[i, 0], ln_g[i, 0], ln_b[i, 0])
        h, hc = modulate(x, m, 1), modulate(xc, mc, 1)
        kind, j = i % N_MIXERS, i // N_MIXERS
        if kind == 0:
            y, yc = gla_mixer(h, hc, gla_w_in[j], gla_w_gdown[j], gla_w_gup[j], gla_b_g[j],
                              gla_norm_g[j], gla_w_out[j], not last)
        elif kind == 1:
            y, yc = mlstm_mixer(h, hc, mlstm_w_in[j], mlstm_w_if[j], mlstm_b_if[j],
                                mlstm_norm_g[j], mlstm_w_out[j], not last)
        else:
            y, yc = mla_mixer(h, hc, mla_w_down[j], mla_q_norm_g[j], mla_w_uq[j],
                              mla_kv_norm_g[j], mla_w_ukv[j], mla_w_out[j], not last)
        x = residual(x, y, m, 1, 1.0, ln_g[i, 1], ln_b[i, 1])
        x = ffn_sublayer(x, m, 2, ffn_w_in[i, 1], ffn_w_out[i, 1], ln_g[i, 2], ln_b[i, 2])
        if not last:
            xc = residual(xc, yc, mc, 1, 1.0, ln_g[i, 1], ln_b[i, 1])
            xc = ffn_sublayer(xc, mc, 2, ffn_w_in[i, 1], ffn_w_out[i, 1], ln_g[i, 2], ln_b[i, 2])
    return x
```

```python
import functools
import math

import numpy as np
import jax
import jax.numpy as jnp
from jax import lax
from jax.experimental import pallas as pl
from jax.experimental.pallas import tpu as pltpu

F32 = jnp.float32
BF16 = jnp.bfloat16

N_MOD = 9
MACARON_WEIGHT = 0.5
LN_EPS = 1e-5
RMS_EPS = 1e-6
CHUNK = 64
H_A = 8
GATE_RANK_A = 16
GATE_NORMALIZER_A = 16.0
H_B = 8
GATE_SOFTCAP = 15.0
M_INIT = -1e30
H_C = 32
D_NOPE = 128
D_ROPE = 64
D_V_C = 128
ROPE_THETA = 10000.0
GRID_W = 64

LANES = 128
MXU_DIM = 256
VMEM_LIMIT_BYTES = 60 * 1024 * 1024
ROW_TILE_TARGET = 512
LN_ROWS = 32

NEG_INF = float("-inf")


def _cparams(*sem):
    return pltpu.CompilerParams(dimension_semantics=sem, vmem_limit_bytes=VMEM_LIMIT_BYTES)


def _pick_tile(n, target, mult=LANES):
    if n <= target:
        return n
    t = (target // mult) * mult
    while t > mult and n % t:
        t -= mult
    assert n % t == 0, (n, target, mult)
    return t


def _sigmoid(x):
    return 1.0 / (1.0 + jnp.exp(-x))


def _log_sigmoid(x):
    return jnp.minimum(x, 0.0) - jnp.log1p(jnp.exp(-jnp.abs(x)))


def _dot(a, b):
    return jnp.dot(a, b, preferred_element_type=F32)


def _dot_nt(a, b):
    return lax.dot_general(a, b, (((1,), (1,)), ((), ())), preferred_element_type=F32)


def _split3(x):
    hi = x.astype(BF16)
    r1 = x - hi.astype(F32)
    mid = r1.astype(BF16)
    lo = (r1 - mid.astype(F32)).astype(BF16)
    return hi, mid, lo


def _mm_kernel(*refs, prologue, epilogue, scale):
    a_ref, w_ref = refs[0], refs[1]
    o_ref = refs[-1]
    extra = refs[2:-1]
    a = a_ref[...]
    if prologue == "rms":
        af = a.astype(F32)
        a = (af * lax.rsqrt(jnp.mean(af * af, axis=-1, keepdims=True) + RMS_EPS) * extra[0][...]).astype(BF16)
        extra = extra[1:]
    elif prologue == "silu":
        af = a.astype(F32)
        a = (af * _sigmoid(af)).astype(BF16)
    acc = _dot(a, w_ref[...])
    if epilogue == "rope":
        cos = extra[0][...]
        sin = extra[1][...]
        lane = lax.broadcasted_iota(jnp.int32, cos.shape, 1)
        first_half = lane < (D_ROPE // 2)
        for g in range(acc.shape[1] // MXU_DIM):
            lo = g * MXU_DIM
            o_ref[:, lo:lo + LANES] = (acc[:, lo:lo + LANES] * scale).astype(o_ref.dtype)
            x = acc[:, lo + LANES:lo + 2 * LANES]
            swapped = jnp.where(first_half, pltpu.roll(x, LANES - D_ROPE // 2, 1), pltpu.roll(x, D_ROPE // 2, 1))
            o_ref[:, lo + LANES:lo + 2 * LANES] = ((x * cos + swapped * sin) * scale).astype(o_ref.dtype)
    else:
        o_ref[...] = acc.astype(o_ref.dtype)


def _mm(a, w, *, out_dtype, k=None, prologue=None, gain=None, epilogue=None, cos=None, sin=None,
        scale=1.0, tm_target=ROW_TILE_TARGET, tn_target=1024, rows=None):
    m = a.shape[0] if rows is None else rows
    kk, n = w.shape
    assert k is None or k == kk
    assert a.shape[1] == kk or a.shape[1] % kk == 0 or kk % LANES == 0
    tm = _pick_tile(m, tm_target, 8)
    tn = _pick_tile(n, tn_target, MXU_DIM if epilogue == "rope" else LANES)
    in_specs = [pl.BlockSpec((tm, kk), lambda j, i: (i, 0)),
                pl.BlockSpec((kk, tn), lambda j, i: (0, j))]
    args = [a, w]
    if prologue == "rms":
        in_specs.append(pl.BlockSpec((1, kk), lambda j, i: (0, 0)))
        args.append(gain.reshape(1, kk).astype(F32))
    if epilogue == "rope":
        in_specs += [pl.BlockSpec((tm, LANES), lambda j, i: (i, 0))] * 2
        args += [cos, sin]
    return pl.pallas_call(
        functools.partial(_mm_kernel, prologue=prologue, epilogue=epilogue, scale=scale),
        out_shape=jax.ShapeDtypeStruct((m, n), out_dtype),
        grid=(n // tn, m // tm),
        in_specs=in_specs,
        out_specs=pl.BlockSpec((tm, tn), lambda j, i: (i, j)),
        compiler_params=_cparams("parallel", "parallel"),
        name="mm_" + (prologue or "p") + "_" + (epilogue or "e"),
    )(*args)


def _ffn1_kernel(h_ref, wa_ref, wu_ref, o_ref):
    h = h_ref[...]
    a = _dot(h, wa_ref[...])
    u = _dot(h, wu_ref[...])
    o_ref[...] = (a * _sigmoid(a) * u).astype(o_ref.dtype)


def _ffn1(h, w_in, rows):
    d, f2 = w_in.shape
    f = f2 // 2
    tm = _pick_tile(rows, ROW_TILE_TARGET, 8)
    tn = _pick_tile(f, 512)
    nj = f // tn
    return pl.pallas_call(
        _ffn1_kernel,
        out_shape=jax.ShapeDtypeStruct((rows, f), BF16),
        grid=(nj, rows // tm),
        in_specs=[pl.BlockSpec((tm, d), lambda j, i: (i, 0)),
                  pl.BlockSpec((d, tn), lambda j, i: (0, j)),
                  pl.BlockSpec((d, tn), lambda j, i: (0, j + nj))],
        out_specs=pl.BlockSpec((tm, tn), lambda j, i: (i, j)),
        compiler_params=_cparams("parallel", "parallel"),
        name="ffn1",
    )(h, w_in, w_in)


def _proj_ln_kernel(act_ref, w_ref, x_ref, gw_ref, lng_ref, lnb_ref, sc_ref, sh_ref, xo_ref, ho_ref, *, alpha):
    kstep = pl.program_id(1)
    part = _dot(act_ref[...], w_ref[...])

    @pl.when(kstep == 0)
    def _():
        xo_ref[...] = part

    @pl.when(kstep > 0)
    def _():
        xo_ref[...] += part

    @pl.when(kstep == pl.num_programs(1) - 1)
    def _():
        gw = gw_ref[...]
        g = lng_ref[...]
        b = lnb_ref[...]
        sc = sc_ref[...]
        sh = sh_ref[...]

        def slab(r, carry):
            rows = pl.ds(pl.multiple_of(r * LN_ROWS, LN_ROWS), LN_ROWS)
            v = alpha * x_ref[rows, :] + gw * xo_ref[rows, :]
            mu = jnp.mean(v, axis=-1, keepdims=True)
            vc = v - mu
            var = jnp.mean(vc * vc, axis=-1, keepdims=True)
            xn = vc * lax.rsqrt(var + LN_EPS) * g + b
            xo_ref[rows, :] = xn
            ho_ref[rows, :] = (xn * sc + sh).astype(ho_ref.dtype)
            return carry

        lax.fori_loop(0, xo_ref.shape[0] // LN_ROWS, slab, 0)


def _proj_ln(act, w, x, gate_w, ln_g, ln_b, scale1p, shift, *, rows, seg_of_tile, tm, alpha):
    kk, d = w.shape
    tk = _pick_tile(kk, 512)
    vec = pl.BlockSpec((None, 1, d), lambda i, k: (seg_of_tile(i), 0, 0))
    cvec = pl.BlockSpec((1, d), lambda i, k: (0, 0))
    return pl.pallas_call(
        functools.partial(_proj_ln_kernel, alpha=alpha),
        out_shape=(jax.ShapeDtypeStruct((rows, d), F32), jax.ShapeDtypeStruct((rows, d), BF16)),
        grid=(rows // tm, kk // tk),
        in_specs=[pl.BlockSpec((tm, tk), lambda i, k: (i, k)),
                  pl.BlockSpec((tk, d), lambda i, k: (k, 0)),
                  pl.BlockSpec((tm, d), lambda i, k: (i, 0)),
                  vec, cvec, cvec, vec, vec],
        out_specs=(pl.BlockSpec((tm, d), lambda i, k: (i, 0)),
                   pl.BlockSpec((tm, d), lambda i, k: (i, 0))),
        compiler_params=_cparams("parallel", "arbitrary"),
        name="proj_ln",
    )(act, w, x, gate_w, ln_g.reshape(1, d), ln_b.reshape(1, d), scale1p, shift)


def _modulate_kernel(x_ref, sc_ref, sh_ref, o_ref):
    o_ref[...] = (x_ref[...] * sc_ref[...] + sh_ref[...]).astype(o_ref.dtype)


def _modulate(x, scale1p, shift, *, seg_of_tile, tm):
    t, d = x.shape
    tr = min(tm, 256)
    ratio = tm // tr
    vec = pl.BlockSpec((None, 1, d), lambda i: (seg_of_tile(i // ratio), 0, 0))
    return pl.pallas_call(
        _modulate_kernel,
        out_shape=jax.ShapeDtypeStruct((t, d), BF16),
        grid=(t // tr,),
        in_specs=[pl.BlockSpec((tr, d), lambda i: (i, 0)), vec, vec],
        out_specs=pl.BlockSpec((tr, d), lambda i: (i, 0)),
        compiler_params=_cparams("parallel"),
        name="modulate",
    )(x, scale1p, shift)


def _chunk_row_block(b, dr, c, *, n_ctx_chunks, n_lat_chunks, n_batch):
    jc = jnp.where(dr == 1, n_ctx_chunks - 1 - c, c)
    jl = jnp.where(dr == 1, n_lat_chunks - 1 - (c - n_ctx_chunks), c - n_ctx_chunks)
    ctx_blk = n_batch * n_lat_chunks + b * n_ctx_chunks + jc
    lat_blk = b * n_lat_chunks + jl
    return jnp.where(c < n_ctx_chunks, ctx_blk, lat_blk)


_GLA_LEVELS = (32, 16, 8, 4, 2, 1)
_GLA_ROWS_B = len(_GLA_LEVELS) * CHUNK
_GLA_ROWS_BL = _GLA_ROWS_B + CHUNK
_GLA_ROWS_ONE = _GLA_ROWS_BL + CHUNK
_GLA_A_ROWS = _GLA_ROWS_ONE + CHUNK


def _gla_constants():
    c = CHUNK
    a = np.zeros((_GLA_A_ROWS, c), np.float32)
    masks = np.zeros((len(_GLA_LEVELS) + 1, c, c), np.float32)
    t = np.arange(c)
    for l, s in enumerate(_GLA_LEVELS):
        for i in range(c):
            r = (i // (2 * s)) * 2 * s + s - 1
            if i % (2 * s) >= s:
                a[l * c + i] = (t > r) & (t <= i)
                lo = (i // (2 * s)) * 2 * s
                masks[l, i, lo:lo + s] = 1.0
            else:
                a[l * c + i] = (t > i) & (t <= r)
    for i in range(c):
        a[_GLA_ROWS_B + i] = t <= i
        a[_GLA_ROWS_BL + i] = t > i
        a[_GLA_ROWS_ONE + i] = 1.0
    masks[len(_GLA_LEVELS)] = np.eye(c)
    a_rev = a.reshape(-1, c, c)[:, ::-1, ::-1].reshape(-1, c)
    masks_rev = masks[:, ::-1, ::-1]
    return (np.stack([a, a_rev]).astype(np.float32), np.stack([masks, masks_rev]).astype(np.float32))


def _gla_kernel(q_ref, k_ref, v_ref, g_ref, wup_ref, bg_ref, a_ref, mask_ref, o_ref, s_ref, *, q_scale):
    dr = pl.program_id(1)
    c = pl.program_id(2)
    h = pl.program_id(3)

    @pl.when(c == 0)
    def _():
        s_ref[h] = jnp.zeros(s_ref.shape[1:], F32)

    qf = q_ref[...].astype(F32) * q_scale
    kf = k_ref[...].astype(F32)
    v = v_ref[...]
    pre = _dot(g_ref[...], wup_ref[...]) + bg_ref[...]
    la = _log_sigmoid(pre) * (1.0 / GATE_NORMALIZER_A)
    hi, mid, lo = _split3(la)
    amat = a_ref[...]
    e = jnp.exp(_dot(amat, hi) + _dot(amat, mid) + _dot(amat, lo))

    row = lax.broadcasted_iota(jnp.int32, (CHUNK, 1), 0)
    row = jnp.where(dr == 1, CHUNK - 1 - row, row)
    p = mask_ref[len(_GLA_LEVELS)] * _dot_nt(qf.astype(BF16), kf.astype(BF16))
    for l, s in enumerate(_GLA_LEVELS):
        upper = (row & s) != 0
        m = (jnp.where(upper, qf, kf) * e[l * CHUNK:(l + 1) * CHUNK]).astype(BF16)
        p = p + mask_ref[l] * _dot_nt(m, m)

    s_old = s_ref[h]
    q_hat = (qf * e[_GLA_ROWS_B:_GLA_ROWS_B + CHUNK]).astype(BF16)
    o_ref[...] = _dot(p.astype(BF16), v) + _dot(q_hat, s_old.astype(BF16))

    k_hat = kf * e[_GLA_ROWS_BL:_GLA_ROWS_BL + CHUNK]
    ke_t = jnp.concatenate([k_hat, e[_GLA_ROWS_ONE:_GLA_ROWS_ONE + CHUNK]], axis=0).T
    decay = jnp.concatenate([ke_t[:, CHUNK:], ke_t[:, CHUNK:]], axis=1)
    v_pad = jnp.concatenate([v, jnp.zeros_like(v)], axis=0)
    upd = _dot(ke_t.astype(BF16), v_pad)
    reps = s_old.shape[1] // LANES
    s_ref[h] = s_old * jnp.concatenate([decay] * reps, axis=1) + upd


def _gla_scan(z, g_low, wup_ext, b_g, *, n_batch, seq, ctx_len, dk, dv):
    t = z.shape[0]
    ncc, nlc = ctx_len // CHUNK, seq // CHUNK
    a_np, mask_np = _gla_constants()
    a_c = jnp.asarray(a_np, BF16)
    mask_c = jnp.asarray(mask_np, F32)
    rb = functools.partial(_chunk_row_block, n_ctx_chunks=ncc, n_lat_chunks=nlc, n_batch=n_batch)
    kblk0 = (H_A * dk) // dk
    vblk0 = (2 * H_A * dk) // dv
    return pl.pallas_call(
        functools.partial(_gla_kernel, q_scale=dk ** -0.5),
        out_shape=jax.ShapeDtypeStruct((2, t, H_A * dv), F32),
        grid=(n_batch, 2, ncc + nlc, H_A),
        in_specs=[pl.BlockSpec((CHUNK, dk), lambda b, dr, c, h: (rb(b, dr, c), h)),
                  pl.BlockSpec((CHUNK, dk), lambda b, dr, c, h: (rb(b, dr, c), kblk0 + h)),
                  pl.BlockSpec((CHUNK, dv), lambda b, dr, c, h: (rb(b, dr, c), vblk0 + h)),
                  pl.BlockSpec((CHUNK, LANES), lambda b, dr, c, h: (rb(b, dr, c), 0)),
                  pl.BlockSpec((None, LANES, dk), lambda b, dr, c, h: (dr, 0, h)),
                  pl.BlockSpec((None, 1, dk), lambda b, dr, c, h: (dr, 0, h)),
                  pl.BlockSpec((None, _GLA_A_ROWS, CHUNK), lambda b, dr, c, h: (dr, 0, 0)),
                  pl.BlockSpec((None, len(_GLA_LEVELS) + 1, CHUNK, CHUNK), lambda b, dr, c, h: (dr, 0, 0, 0))],
        out_specs=pl.BlockSpec((None, CHUNK, dv), lambda b, dr, c, h: (dr, rb(b, dr, c), h)),
        scratch_shapes=[pltpu.VMEM((H_A, dk, dv), F32)],
        compiler_params=_cparams("arbitrary", "arbitrary", "arbitrary", "arbitrary"),
        name="gla_scan",
    )(z, z, z, g_low, wup_ext, b_g, a_c, mask_c)


def _mlstm_kernel(q_ref, k_ref, v_ref, gi_ref, gf_ref, o_ref, cn_ref, m_ref, *, q_scale, dv):
    dr = pl.program_id(1)
    c = pl.program_id(2)
    h = pl.program_id(3)

    @pl.when(c == 0)
    def _():
        cn_ref[h] = jnp.zeros(cn_ref.shape[1:], F32)
        m_ref[h] = jnp.full(m_ref.shape[1:], M_INIT, F32)

    q = (q_ref[...].astype(F32) * q_scale).astype(BF16)
    k = k_ref[...]
    v = v_ref[...]
    ic = GATE_SOFTCAP * jnp.tanh(gi_ref[...] * (1.0 / GATE_SOFTCAP))
    fc = _log_sigmoid(GATE_SOFTCAP * jnp.tanh(gf_ref[...] * (1.0 / GATE_SOFTCAP)))

    ii = lax.broadcasted_iota(jnp.int32, (CHUNK, CHUNK), 0)
    jj = lax.broadcasted_iota(jnp.int32, (CHUNK, CHUNK), 1)
    eye = ii == jj
    causal = (jj - ii) * (1 - 2 * dr) <= 0

    def to_col(r):
        return jnp.sum(jnp.where(eye, jnp.broadcast_to(r, (CHUNK, CHUNK)), 0.0), axis=1, keepdims=True)

    def to_row(cl):
        return jnp.sum(jnp.where(eye, jnp.broadcast_to(cl, (CHUNK, CHUNK)), 0.0), axis=0, keepdims=True)

    b_col = jnp.sum(jnp.where(causal, jnp.broadcast_to(fc, (CHUNK, CHUNK)), 0.0), axis=1, keepdims=True)
    b_row = to_row(b_col)
    m_old = m_ref[h][0:1, 0:1]
    d_log = jnp.where(causal, b_col - b_row + ic, NEG_INF)
    inter_log = b_col + m_old
    m_row = jnp.maximum(inter_log, jnp.max(d_log, axis=1, keepdims=True))
    w_intra = jnp.exp(d_log - m_row)
    w_inter = jnp.exp(inter_log - m_row)

    lane = lax.broadcasted_iota(jnp.int32, (CHUNK, LANES), 1)
    ones_col = jnp.where(lane == 0, 1.0, 0.0).astype(BF16)
    v_ext = jnp.concatenate([v, ones_col], axis=1)
    s = _dot_nt(q, k) * w_intra
    cn_old = cn_ref[h]
    nd = _dot(s.astype(BF16), v_ext) + w_inter * _dot(q, cn_old.astype(BF16))
    den = nd[:, dv:dv + 1]
    o_ref[...] = nd[:, :dv] / jnp.maximum(jnp.abs(den), jnp.exp(-m_row))

    b_last = jnp.sum(fc, axis=1, keepdims=True)
    w_log = b_last - b_row + ic
    m_new = jnp.maximum(b_last + m_old, jnp.max(w_log, axis=1, keepdims=True))
    keep = jnp.exp(b_last + m_old - m_new)
    w_col = to_col(jnp.exp(w_log - m_new))
    wk = k.astype(F32) * w_col
    wk_t = jnp.concatenate([wk, jnp.zeros_like(wk)], axis=0).T
    v_pad = jnp.concatenate([v_ext, jnp.zeros_like(v_ext)], axis=0)
    cn_ref[h] = keep * cn_old + _dot(wk_t.astype(BF16), v_pad)
    m_ref[h] = jnp.broadcast_to(m_new, m_ref.shape[1:])


def _mlstm_scan(z, gates, *, n_batch, seq, ctx_len, dk, dv):
    t = z.shape[0]
    ncc, nlc = ctx_len // CHUNK, seq // CHUNK
    rb = functools.partial(_chunk_row_block, n_ctx_chunks=ncc, n_lat_chunks=nlc, n_batch=n_batch)
    kblk0 = (H_B * dk) // dk
    vblk0 = (2 * H_B * dk) // dv
    return pl.pallas_call(
        functools.partial(_mlstm_kernel, q_scale=dk ** -0.5, dv=dv),
        out_shape=jax.ShapeDtypeStruct((2, t, H_B * dv), F32),
        grid=(n_batch, 2, ncc + nlc, H_B),
        in_specs=[pl.BlockSpec((CHUNK, dk), lambda b, dr, c, h: (rb(b, dr, c), h)),
                  pl.BlockSpec((CHUNK, dk), lambda b, dr, c, h: (rb(b, dr, c), kblk0 + h)),
                  pl.BlockSpec((CHUNK, dv), lambda b, dr, c, h: (rb(b, dr, c), vblk0 + h)),
                  pl.BlockSpec((None, None, None, 1, CHUNK), lambda b, dr, c, h: (dr, h, rb(b, dr, c), 0, 0)),
                  pl.BlockSpec((None, None, None, 1, CHUNK), lambda b, dr, c, h: (dr, H_B + h, rb(b, dr, c), 0, 0))],
        out_specs=pl.BlockSpec((None, CHUNK, dv), lambda b, dr, c, h: (dr, rb(b, dr, c), h)),
        scratch_shapes=[pltpu.VMEM((H_B, dk, dv + LANES), F32), pltpu.VMEM((H_B, 8, LANES), F32)],
        compiler_params=_cparams("arbitrary", "arbitrary", "arbitrary", "arbitrary"),
        name="mlstm_scan",
    )(z, z, z, gates, gates)


def _readout_kernel(o_ref, g_ref, ng_ref, y_ref, *, n_heads, gate):
    o = o_ref[0] + o_ref[1]
    g = g_ref[...].astype(F32)
    act = g * _sigmoid(g) if gate == "silu" else _sigmoid(g)
    dh = o.shape[1] // n_heads
    for hh in range(n_heads):
        sl = slice(hh * dh, (hh + 1) * dh)
        oh = o[:, sl]
        r = lax.rsqrt(jnp.mean(oh * oh, axis=-1, keepdims=True) + RMS_EPS)
        y_ref[:, sl] = (oh * r * ng_ref[:, sl] * act[:, sl]).astype(y_ref.dtype)


def _readout(o2, z, norm_g, *, n_heads, gate, gate_col0, rows):
    d = o2.shape[2]
    tr = _pick_tile(rows, 128, 8)
    gblk = gate_col0 // d
    return pl.pallas_call(
        functools.partial(_readout_kernel, n_heads=n_heads, gate=gate),
        out_shape=jax.ShapeDtypeStruct((rows, d), BF16),
        grid=(rows // tr,),
        in_specs=[pl.BlockSpec((2, tr, d), lambda i: (0, i, 0)),
                  pl.BlockSpec((tr, d), lambda i: (i, gblk)),
                  pl.BlockSpec((1, d), lambda i: (0, 0))],
        out_specs=pl.BlockSpec((tr, d), lambda i: (i, 0)),
        compiler_params=_cparams("parallel"),
        name="readout",
    )(o2, z, norm_g.reshape(1, d))


def _flash_kernel(*refs, n_seg, tk):
    q_ref = refs[0]
    kv = refs[1:1 + 2 * n_seg]
    o_ref = refs[1 + 2 * n_seg]
    m_ref, l_ref, acc_ref = refs[2 + 2 * n_seg:]
    q = q_ref[...]
    m_ref[...] = jnp.full(m_ref.shape, NEG_INF, F32)
    l_ref[...] = jnp.zeros(l_ref.shape, F32)
    acc_ref[...] = jnp.zeros(acc_ref.shape, F32)

    def block(k, v):
        s = _dot_nt(q, k)
        m_old = m_ref[...]
        m_new = jnp.maximum(m_old, jnp.max(s, axis=-1, keepdims=True))
        alpha = jnp.exp(m_old - m_new)
        p = jnp.exp(s - m_new)
        l_ref[...] = alpha * l_ref[...] + jnp.sum(p, axis=-1, keepdims=True)
        acc_ref[...] = alpha * acc_ref[...] + _dot(p.astype(BF16), v)
        m_ref[...] = m_new

    for sg in range(n_seg):
        k_ref, v_ref = kv[2 * sg], kv[2 * sg + 1]
        length = k_ref.shape[0]
        tkk = min(tk, length)
        if length == tkk:
            block(k_ref[...], v_ref[...])
        else:
            def body(t, carry, k_ref=k_ref, v_ref=v_ref, tkk=tkk):
                rows = pl.ds(pl.multiple_of(t * tkk, tkk), tkk)
                block(k_ref[rows, :], v_ref[rows, :])
                return carry
            lax.fori_loop(0, length // tkk, body, 0)
    o_ref[...] = (acc_ref[...] / l_ref[...]).astype(o_ref.dtype)


def _flash(qcat, kcat, vmat, *, n_batch, q_rows0, q_len, segs, out_rows0):
    tq = _pick_tile(q_len, 512, 8)
    nq = q_len // tq
    in_specs = [pl.BlockSpec((tq, MXU_DIM), lambda b, h, i: ((q_rows0 + b * q_len) // tq + i, h))]
    args = [qcat]
    for row0, length in segs:
        in_specs.append(pl.BlockSpec((length, MXU_DIM), lambda b, h, i, row0=row0, length=length: (row0 // length + b, h)))
        in_specs.append(pl.BlockSpec((length, D_V_C), lambda b, h, i, row0=row0, length=length: (row0 // length + b, h)))
        args += [kcat, vmat]
    return pl.pallas_call(
        functools.partial(_flash_kernel, n_seg=len(segs), tk=512),
        out_shape=jax.ShapeDtypeStruct((n_batch * q_len, H_C * D_V_C), BF16),
        grid=(n_batch, H_C, nq),
        in_specs=in_specs,
        out_specs=pl.BlockSpec((tq, D_V_C), lambda b, h, i: (b * nq + i, h)),
        scratch_shapes=[pltpu.VMEM((tq, 1), F32), pltpu.VMEM((tq, 1), F32), pltpu.VMEM((tq, D_V_C), F32)],
        compiler_params=_cparams("parallel", "parallel", "arbitrary"),
        name="flash",
    )(*args)


def _mla_prep_kernel(z_ref, gq_ref, gkv_ref, cos_ref, sin_ref, cq_ref, akv_ref, *, q_rank, kv_rank):
    cq = z_ref[:, :q_rank]
    cq_ref[...] = (cq * lax.rsqrt(jnp.mean(cq * cq, axis=-1, keepdims=True) + RMS_EPS) * gq_ref[...]).astype(BF16)
    ckv = z_ref[:, q_rank:q_rank + kv_rank]
    akv_ref[:, :kv_rank] = (ckv * lax.rsqrt(jnp.mean(ckv * ckv, axis=-1, keepdims=True) + RMS_EPS)
                            * gkv_ref[...]).astype(BF16)
    x = z_ref[:, q_rank + kv_rank:q_rank + kv_rank + LANES]
    lane = lax.broadcasted_iota(jnp.int32, x.shape, 1)
    swapped = jnp.where(lane < D_ROPE // 2, pltpu.roll(x, LANES - D_ROPE // 2, 1), pltpu.roll(x, D_ROPE // 2, 1))
    akv_ref[:, kv_rank:] = (x * cos_ref[...] + swapped * sin_ref[...]).astype(BF16)


def _mla_prep(zd, gq, gkv, cos, sin, *, q_rank, kv_rank):
    t = zd.shape[0]
    tr = _pick_tile(t, 256, 8)
    return pl.pallas_call(
        functools.partial(_mla_prep_kernel, q_rank=q_rank, kv_rank=kv_rank),
        out_shape=(jax.ShapeDtypeStruct((t, q_rank), BF16), jax.ShapeDtypeStruct((t, kv_rank + LANES), BF16)),
        grid=(t // tr,),
        in_specs=[pl.BlockSpec((tr, zd.shape[1]), lambda i: (i, 0)),
                  pl.BlockSpec((1, q_rank), lambda i: (0, 0)),
                  pl.BlockSpec((1, kv_rank), lambda i: (0, 0)),
                  pl.BlockSpec((tr, LANES), lambda i: (i, 0)),
                  pl.BlockSpec((tr, LANES), lambda i: (i, 0))],
        out_specs=(pl.BlockSpec((tr, q_rank), lambda i: (i, 0)),
                   pl.BlockSpec((tr, kv_rank + LANES), lambda i: (i, 0))),
        compiler_params=_cparams("parallel"),
        name="mla_prep",
    )(zd, gq.reshape(1, q_rank), gkv.reshape(1, kv_rank), cos, sin)


def _rope_tables(n_batch, seq, ctx_len):
    n_rows = seq // GRID_W
    rows = np.repeat(np.arange(n_rows, dtype=np.float32), GRID_W)
    cols = np.tile(np.arange(GRID_W, dtype=np.float32), n_rows)
    half = D_ROPE // 2
    inv_freq = ROPE_THETA ** (-jnp.arange(0, half, 2, dtype=F32) / half)
    ang = jnp.concatenate([jnp.asarray(rows)[:, None] * inv_freq, jnp.asarray(cols)[:, None] * inv_freq], axis=-1)
    cos, sin = jnp.cos(ang), jnp.sin(ang)
    pad1 = jnp.ones((seq, LANES - D_ROPE), F32)
    pad0 = jnp.zeros((seq, LANES - D_ROPE), F32)
    cos_l = jnp.concatenate([cos, cos, pad1], axis=1)
    sin_l = jnp.concatenate([-sin, sin, pad0], axis=1)
    cos_t = jnp.concatenate([jnp.tile(cos_l, (n_batch, 1)), jnp.ones((n_batch * ctx_len, LANES), F32)], axis=0)
    sin_t = jnp.concatenate([jnp.tile(sin_l, (n_batch, 1)), jnp.zeros((n_batch * ctx_len, LANES), F32)], axis=0)
    return cos_t, sin_t


def _gla_mixer(h, w_in, w_gdown, w_gup, b_g, norm_g, dims):
    n_batch, seq, ctx_len, d = dims
    dk, dv = d // (2 * H_A), d // H_A
    qk = H_A * dk
    z = _mm(h, w_in.astype(BF16), out_dtype=BF16)
    w_low = jnp.concatenate([w_gdown[0], w_gdown[1],
                             jnp.zeros((d, LANES - 2 * GATE_RANK_A), F32)], axis=1).astype(BF16)
    g_low = _mm(h, w_low, out_dtype=BF16)
    wup = jnp.zeros((2, LANES, qk), F32)
    wup = wup.at[0, :GATE_RANK_A].set(w_gup[0]).at[1, GATE_RANK_A:2 * GATE_RANK_A].set(w_gup[1]).astype(BF16)
    o2 = _gla_scan(z, g_low, wup, b_g.reshape(2, 1, qk), n_batch=n_batch, seq=seq, ctx_len=ctx_len, dk=dk, dv=dv)
    return _readout(o2, z, norm_g, n_heads=H_A, gate="silu", gate_col0=2 * qk + d, rows=z.shape[0])


def _mlstm_mixer(h, w_in, w_if, b_if, norm_g, dims):
    n_batch, seq, ctx_len, d = dims
    dk, dv = d // (2 * H_B), d // H_B
    qk = H_B * dk
    t = h.shape[0]
    z = _mm(h, w_in.astype(BF16), out_dtype=BF16)
    w_g = jnp.concatenate([w_if[0], w_if[1], jnp.zeros((d, LANES - 4 * H_B), F32)], axis=1).astype(BF16)
    pre = _mm(h, w_g, out_dtype=F32)[:, :4 * H_B] + jnp.concatenate([b_if[0], b_if[1]])[None, :]
    gates = pre.T.reshape(2, 2 * H_B, t // CHUNK, 1, CHUNK)
    o2 = _mlstm_scan(z, gates, n_batch=n_batch, seq=seq, ctx_len=ctx_len, dk=dk, dv=dv)
    return _readout(o2, z, norm_g, n_heads=H_B, gate="sigmoid", gate_col0=2 * qk + d, rows=t)


def _mla_mixer(h, w_down, q_norm_g, w_uq, kv_norm_g, w_ukv, dims, need_ctx_out):
    n_batch, seq, ctx_len, d = dims
    q_rank, kv_rank = d // 4, d // 8
    t = h.shape[0]
    n_lat = n_batch * seq
    scale = (D_NOPE + D_ROPE) ** -0.5
    cos_t, sin_t = _rope_tables(n_batch, seq, ctx_len)

    n_down = q_rank + kv_rank + LANES
    w_down_ext = jnp.concatenate([w_down, jnp.zeros((d, n_down - w_down.shape[1]), F32)], axis=1).astype(BF16)
    zd = _mm(h, w_down_ext, out_dtype=F32, tn_target=2048)
    cqn, akv = _mla_prep(zd, q_norm_g, kv_norm_g, cos_t, sin_t, q_rank=q_rank, kv_rank=kv_rank)

    wq = w_uq.reshape(q_rank, H_C, D_NOPE + D_ROPE)
    wq = jnp.concatenate([wq, jnp.zeros((q_rank, H_C, MXU_DIM - D_NOPE - D_ROPE), F32)], axis=2)
    qcat = _mm(cqn, wq.reshape(q_rank, H_C * MXU_DIM).astype(BF16), out_dtype=BF16,
               epilogue="rope", cos=cos_t, sin=sin_t, scale=scale)

    wkv = w_ukv.reshape(kv_rank, H_C, D_NOPE + D_V_C)
    wk_top = jnp.concatenate([wkv[:, :, :D_NOPE], jnp.zeros((kv_rank, H_C, MXU_DIM - D_NOPE), F32)], axis=2)
    ident = jnp.concatenate([jnp.zeros((LANES, D_NOPE), F32), jnp.eye(LANES, dtype=F32)], axis=1)
    wk_ext = jnp.concatenate([wk_top, jnp.broadcast_to(ident[:, None, :], (LANES, H_C, MXU_DIM))], axis=0)
    kcat = _mm(akv, wk_ext.reshape(kv_rank + LANES, H_C * MXU_DIM).astype(BF16), out_dtype=BF16)
    wv = jnp.concatenate([wkv[:, :, D_NOPE:].reshape(kv_rank, H_C * D_V_C),
                          jnp.zeros((LANES, H_C * D_V_C), F32)], axis=0)
    vmat = _mm(akv, wv.astype(BF16), out_dtype=BF16)

    o_lat = _flash(qcat, kcat, vmat, n_batch=n_batch, q_rows0=0, q_len=seq,
                   segs=[(0, seq), (n_lat, ctx_len)], out_rows0=0)
    if not need_ctx_out:
        return o_lat
    o_ctx = _flash(qcat, kcat, vmat, n_batch=n_batch, q_rows0=n_lat, q_len=ctx_len,
                   segs=[(n_lat, ctx_len)], out_rows0=0)
    return jnp.concatenate([o_lat, o_ctx], axis=0)


def _modulations(cond, ada_down, ada_up, ada_bias):
    d = cond.shape[1]
    low = _mm(cond, ada_down.astype(BF16), out_dtype=BF16, prologue="silu")
    m = _mm(low, ada_up.astype(BF16), out_dtype=F32) + ada_bias[None, :]
    return m.reshape(cond.shape[0], N_MOD, d)


def kernel(x, c, ctx, c_ctx, ada_down, ada_up, ada_bias, ln_g, ln_b, ffn_w_in, ffn_w_out, gla_w_in, gla_w_gdown, gla_w_gup, gla_b_g, gla_norm_g, gla_w_out, mlstm_w_in, mlstm_w_if, mlstm_b_if, mlstm_norm_g, mlstm_w_out, mla_w_down, mla_q_norm_g, mla_w_uq, mla_kv_norm_g, mla_w_ukv, mla_w_out):
    n_batch, seq, d = x.shape
    ctx_len = ctx.shape[1]
    depth = ada_down.shape[0]
    n_lat = n_batch * seq
    n_ctx = n_batch * ctx_len
    t_all = n_lat + n_ctx
    dims = (n_batch, seq, ctx_len, d)
    alpha = (2.0 * depth) ** 0.25
    tm = math.gcd(math.gcd(seq, n_ctx), ROW_TILE_TARGET)
    n_seg = n_batch + 1

    def seg_of_tile(i):
        return jnp.minimum((i * tm) // seq, n_batch)

    xs = jnp.concatenate([x.reshape(n_lat, d), ctx.reshape(n_ctx, d)], axis=0)
    cond = jnp.concatenate([c, c_ctx[None, :], jnp.zeros((8 - n_seg, d), F32)], axis=0)

    def mod_vecs(m, s, weight):
        shift = m[:n_seg, 3 * s][:, None, :]
        scale1p = 1.0 + m[:n_seg, 3 * s + 1][:, None, :]
        gate_w = weight * m[:n_seg, 3 * s + 2][:, None, :]
        return shift, scale1p, gate_w

    mods = [_modulations(cond, ada_down[i], ada_up[i], ada_bias[i]) for i in range(depth)]
    shift0, scale0, _ = mod_vecs(mods[0], 0, MACARON_WEIGHT)
    hcur = _modulate(xs, scale0, shift0, seg_of_tile=seg_of_tile, tm=tm)

    for i in range(depth):
        last = i == depth - 1
        m = mods[i]
        sh0, sc0, gw0 = mod_vecs(m, 0, MACARON_WEIGHT)
        sh1, sc1, gw1 = mod_vecs(m, 1, 1.0)
        sh2, sc2, gw2 = mod_vecs(m, 2, MACARON_WEIGHT)
        pln = functools.partial(_proj_ln, seg_of_tile=seg_of_tile, tm=tm, alpha=alpha)

        act = _ffn1(hcur, ffn_w_in[i, 0].astype(BF16), t_all)
        xs, hcur = pln(act, ffn_w_out[i, 0].astype(BF16), xs, gw0, ln_g[i, 0], ln_b[i, 0], sc1, sh1, rows=t_all)

        kind, j = i % 3, i // 3
        rows1 = n_lat if last else t_all
        if kind == 0:
            y_in = _gla_mixer(hcur, gla_w_in[j], gla_w_gdown[j], gla_w_gup[j], gla_b_g[j], gla_norm_g[j], dims)
            w_o = gla_w_out[j]
        elif kind == 1:
            y_in = _mlstm_mixer(hcur, mlstm_w_in[j], mlstm_w_if[j], mlstm_b_if[j], mlstm_norm_g[j], dims)
            w_o = mlstm_w_out[j]
        else:
            y_in = _mla_mixer(hcur, mla_w_down[j], mla_q_norm_g[j], mla_w_uq[j], mla_kv_norm_g[j], mla_w_ukv[j],
                              dims, not last)
            w_o = mla_w_out[j]
        xs, hcur = pln(y_in, w_o.astype(BF16), xs, gw1, ln_g[i, 1], ln_b[i, 1], sc2, sh2, rows=rows1)

        if last:
            sh_n, sc_n = sh2, sc2
        else:
            sh_n, sc_n, _ = mod_vecs(mods[i + 1], 0, MACARON_WEIGHT)
        act = _ffn1(hcur, ffn_w_in[i, 1].astype(BF16), rows1)
        xs, hcur = pln(act, ffn_w_out[i, 1].astype(BF16), xs, gw2, ln_g[i, 2], ln_b[i, 2], sc_n, sh_n, rows=rows1)

    return xs.reshape(n_batch, seq, d)
```

```python
import functools
import math

import numpy as np
import jax
import jax.numpy as jnp
from jax import lax
from jax.experimental import pallas as pl
from jax.experimental.pallas import tpu as pltpu

F32 = jnp.float32
BF16 = jnp.bfloat16

N_MOD = 9
MACARON_WEIGHT = 0.5
LN_EPS = 1e-5
RMS_EPS = 1e-6
CHUNK = 64
H_A = 8
GATE_RANK_A = 16
GATE_NORMALIZER_A = 16.0
H_B = 8
GATE_SOFTCAP = 15.0
M_INIT = -1e30
H_C = 32
D_NOPE = 128
D_ROPE = 64
D_V_C = 128
ROPE_THETA = 10000.0
GRID_W = 64

LANES = 128
MXU_DIM = 256
VMEM_LIMIT_BYTES = 60 * 1024 * 1024
ROW_TILE_TARGET = 512
LN_ROWS = 32
KV_TILE = 512
ONES_ROWS = 16

NEG_INF = float("-inf")
LOG2_E = math.log2(math.e)


def _cparams(*sem):
    return pltpu.CompilerParams(dimension_semantics=sem, vmem_limit_bytes=VMEM_LIMIT_BYTES)


def _pick_tile(n, target, mult=LANES):
    if n <= target:
        return n
    t = (target // mult) * mult
    while t > mult and n % t:
        t -= mult
    assert n % t == 0, (n, target, mult)
    return t


def _sigmoid(x):
    return 1.0 / (1.0 + jnp.exp(-x))


def _log_sigmoid(x):
    return jnp.minimum(x, 0.0) - jnp.log1p(jnp.exp(-jnp.abs(x)))


def _dot(a, b):
    return jnp.dot(a, b, preferred_element_type=F32)


def _dot_nt(a, b):
    return lax.dot_general(a, b, (((1,), (1,)), ((), ())), preferred_element_type=F32)


def _split2(x):
    hi = x.astype(BF16)
    lo = (x - hi.astype(F32)).astype(BF16)
    return hi, lo


def _wspec(lead, block, index_map):
    return pl.BlockSpec((None,) * len(lead) + block, lambda *g: tuple(lead) + index_map(*g))


def _mm_kernel(*refs, prologue, epilogue, scale):
    a_ref, w_ref = refs[0], refs[1]
    o_ref = refs[-1]
    extra = refs[2:-1]
    a = a_ref[...]
    if prologue == "silu":
        af = a.astype(F32)
        a = (af * _sigmoid(af)).astype(BF16)
    acc = _dot(a, w_ref[...])
    if epilogue == "rope":
        cos = extra[0][...]
        sin = extra[1][...]
        lane = lax.broadcasted_iota(jnp.int32, cos.shape, 1)
        first_half = lane < (D_ROPE // 2)
        for g in range(acc.shape[1] // MXU_DIM):
            lo = g * MXU_DIM
            o_ref[:, lo:lo + LANES] = (acc[:, lo:lo + LANES] * scale).astype(o_ref.dtype)
            x = acc[:, lo + LANES:lo + 2 * LANES]
            swapped = jnp.where(first_half, pltpu.roll(x, LANES - D_ROPE // 2, 1), pltpu.roll(x, D_ROPE // 2, 1))
            o_ref[:, lo + LANES:lo + 2 * LANES] = ((x * cos + swapped * sin) * scale).astype(o_ref.dtype)
    else:
        o_ref[...] = acc.astype(o_ref.dtype)


def _mm(a, w, *, out_dtype, w_lead=(), prologue=None, epilogue=None, cos=None, sin=None,
        scale=1.0, tm_target=ROW_TILE_TARGET, tn_target=1024):
    m = a.shape[0]
    kk, n = w.shape[-2:]
    assert a.shape[1] == kk or kk % LANES == 0
    tm = _pick_tile(m, tm_target, 8)
    tn = _pick_tile(n, tn_target, MXU_DIM if epilogue == "rope" else LANES)
    in_specs = [pl.BlockSpec((tm, kk), lambda j, i: (i, 0)),
                _wspec(w_lead, (kk, tn), lambda j, i: (0, j))]
    args = [a, w]
    if epilogue == "rope":
        in_specs += [pl.BlockSpec((tm, LANES), lambda j, i: (i, 0))] * 2
        args += [cos, sin]
    return pl.pallas_call(
        functools.partial(_mm_kernel, prologue=prologue, epilogue=epilogue, scale=scale),
        out_shape=jax.ShapeDtypeStruct((m, n), out_dtype),
        grid=(n // tn, m // tm),
        in_specs=in_specs,
        out_specs=pl.BlockSpec((tm, tn), lambda j, i: (i, j)),
        compiler_params=_cparams("parallel", "parallel"),
        name="mm_" + (prologue or "p") + "_" + (epilogue or "e"),
    )(*args)


def _ffn1_kernel(h_ref, wa_ref, wu_ref, o_ref):
    h = h_ref[...]
    a = _dot(h, wa_ref[...])
    u = _dot(h, wu_ref[...])
    o_ref[...] = (a * _sigmoid(a) * u).astype(o_ref.dtype)


def _ffn1(h, w_in, rows, w_lead=()):
    d, f2 = w_in.shape[-2:]
    f = f2 // 2
    tm = _pick_tile(rows, ROW_TILE_TARGET, 8)
    tn = _pick_tile(f, 512)
    nj = f // tn
    return pl.pallas_call(
        _ffn1_kernel,
        out_shape=jax.ShapeDtypeStruct((rows, f), BF16),
        grid=(nj, rows // tm),
        in_specs=[pl.BlockSpec((tm, d), lambda j, i: (i, 0)),
                  _wspec(w_lead, (d, tn), lambda j, i: (0, j)),
                  _wspec(w_lead, (d, tn), lambda j, i: (0, j + nj))],
        out_specs=pl.BlockSpec((tm, tn), lambda j, i: (i, j)),
        compiler_params=_cparams("parallel", "parallel"),
        name="ffn1",
    )(h, w_in, w_in)


def _proj_ln_kernel(act_ref, w_ref, x_ref, gw_ref, lng_ref, lnb_ref, sc_ref, sh_ref, xo_ref, ho_ref, *, alpha):
    kstep = pl.program_id(1)

    @pl.when(kstep == 0)
    def _():
        xo_ref[...] = _dot(act_ref[...], w_ref[...])

    @pl.when(kstep > 0)
    def _():
        xo_ref[...] = _dot(act_ref[...], w_ref[...]) + xo_ref[...]

    @pl.when(kstep == pl.num_programs(1) - 1)
    def _():
        gw = gw_ref[...]
        g = lng_ref[...]
        b = lnb_ref[...]
        sc = sc_ref[...]
        sh = sh_ref[...]

        def slab(r, carry):
            rows = pl.ds(pl.multiple_of(r * LN_ROWS, LN_ROWS), LN_ROWS)
            v = alpha * x_ref[rows, :] + gw * xo_ref[rows, :]
            mu = jnp.mean(v, axis=-1, keepdims=True)
            vc = v - mu
            var = jnp.mean(vc * vc, axis=-1, keepdims=True)
            xn = vc * lax.rsqrt(var + LN_EPS) * g + b
            xo_ref[rows, :] = xn
            ho_ref[rows, :] = (xn * sc + sh).astype(ho_ref.dtype)
            return carry

        lax.fori_loop(0, xo_ref.shape[0] // LN_ROWS, slab, 0)


def _proj_ln(act, w, x, gate_w, ln_g, ln_b, scale1p, shift, *, rows, seg_of_tile, tm, alpha, w_lead=()):
    kk, d = w.shape[-2:]
    tk = _pick_tile(kk, 512)
    vec = pl.BlockSpec((None, 1, d), lambda i, k: (seg_of_tile(i), 0, 0))
    cvec = pl.BlockSpec((1, d), lambda i, k: (0, 0))
    return pl.pallas_call(
        functools.partial(_proj_ln_kernel, alpha=alpha),
        out_shape=(jax.ShapeDtypeStruct((rows, d), F32), jax.ShapeDtypeStruct((rows, d), BF16)),
        grid=(rows // tm, kk // tk),
        in_specs=[pl.BlockSpec((tm, tk), lambda i, k: (i, k)),
                  _wspec(w_lead, (tk, d), lambda i, k: (k, 0)),
                  pl.BlockSpec((tm, d), lambda i, k: (i, 0)),
                  vec, cvec, cvec, vec, vec],
        out_specs=(pl.BlockSpec((tm, d), lambda i, k: (i, 0)),
                   pl.BlockSpec((tm, d), lambda i, k: (i, 0))),
        compiler_params=_cparams("parallel", "arbitrary"),
        name="proj_ln",
    )(act, w, x, gate_w, ln_g.reshape(1, d), ln_b.reshape(1, d), scale1p, shift)


def _modulate_kernel(x_ref, sc_ref, sh_ref, o_ref):
    o_ref[...] = (x_ref[...] * sc_ref[...] + sh_ref[...]).astype(o_ref.dtype)


def _modulate(x, scale1p, shift, *, seg_of_tile, tm):
    t, d = x.shape
    tr = min(tm, 256)
    ratio = tm // tr
    vec = pl.BlockSpec((None, 1, d), lambda i: (seg_of_tile(i // ratio), 0, 0))
    return pl.pallas_call(
        _modulate_kernel,
        out_shape=jax.ShapeDtypeStruct((t, d), BF16),
        grid=(t // tr,),
        in_specs=[pl.BlockSpec((tr, d), lambda i: (i, 0)), vec, vec],
        out_specs=pl.BlockSpec((tr, d), lambda i: (i, 0)),
        compiler_params=_cparams("parallel"),
        name="modulate",
    )(x, scale1p, shift)


def _chunk_row_block(b, dr, c, *, n_ctx_chunks, n_lat_chunks, n_batch):
    jc = jnp.where(dr == 1, n_ctx_chunks - 1 - c, c)
    jl = jnp.where(dr == 1, n_lat_chunks - 1 - (c - n_ctx_chunks), c - n_ctx_chunks)
    ctx_blk = n_batch * n_lat_chunks + b * n_ctx_chunks + jc
    lat_blk = b * n_lat_chunks + jl
    return jnp.where(c < n_ctx_chunks, ctx_blk, lat_blk)


_GLA_LEVELS = (32, 16, 8, 4, 2, 1)
_GLA_ROWS_B = len(_GLA_LEVELS) * CHUNK
_GLA_ROWS_BL = _GLA_ROWS_B + CHUNK
_GLA_ROWS_ONE = _GLA_ROWS_BL + CHUNK
_GLA_A_ROWS = _GLA_ROWS_ONE + CHUNK


def _gla_constants():
    c = CHUNK
    a = np.zeros((_GLA_A_ROWS, c), np.float32)
    masks = np.zeros((len(_GLA_LEVELS) + 1, c, c), np.float32)
    t = np.arange(c)
    for l, s in enumerate(_GLA_LEVELS):
        for i in range(c):
            r = (i // (2 * s)) * 2 * s + s - 1
            if i % (2 * s) >= s:
                a[l * c + i] = (t > r) & (t <= i)
                lo = (i // (2 * s)) * 2 * s
                masks[l, i, lo:lo + s] = 1.0
            else:
                a[l * c + i] = (t > i) & (t <= r)
    for i in range(c):
        a[_GLA_ROWS_B + i] = t <= i
        a[_GLA_ROWS_BL + i] = t > i
        a[_GLA_ROWS_ONE + i] = 1.0
    masks[len(_GLA_LEVELS)] = np.eye(c)
    a_rev = a.reshape(-1, c, c)[:, ::-1, ::-1].reshape(-1, c)
    masks_rev = masks[:, ::-1, ::-1]
    return (np.stack([a, a_rev]).astype(np.float32), np.stack([masks, masks_rev]).astype(np.float32))


def _gla_kernel(q_ref, k_ref, v_ref, g_ref, wup_ref, bg_ref, a_ref, mask_ref, o_ref, s_ref, *,
                q_scale, n_heads, dk, dv):
    dr = pl.program_id(1)
    c = pl.program_id(2)

    @pl.when(c == 0)
    def _():
        s_ref[...] = jnp.zeros(s_ref.shape, F32)

    pre = _dot(g_ref[...], wup_ref[...]) + bg_ref[...]
    la_all = _log_sigmoid(pre) * (1.0 / GATE_NORMALIZER_A)
    amat = a_ref[...]
    row = lax.broadcasted_iota(jnp.int32, (CHUNK, 1), 0)
    row = jnp.where(dr == 1, CHUNK - 1 - row, row)
    reps = dv // LANES

    def exponents(h):
        hi, lo = _split2(la_all[:, h * dk:(h + 1) * dk])
        return jnp.exp(_dot(amat, hi) + _dot(amat, lo))

    def intra(h, e):
        ks = slice(h * dk, (h + 1) * dk)
        qf = q_ref[:, ks].astype(F32) * q_scale
        kf = k_ref[:, ks].astype(F32)
        p = mask_ref[len(_GLA_LEVELS)] * _dot_nt(qf.astype(BF16), kf.astype(BF16))
        for l, s in enumerate(_GLA_LEVELS):
            upper = (row & s) != 0
            m = (jnp.where(upper, qf, kf) * e[l * CHUNK:(l + 1) * CHUNK]).astype(BF16)
            p = p + mask_ref[l] * _dot_nt(m, m)
        q_hat = (qf * e[_GLA_ROWS_B:_GLA_ROWS_B + CHUNK]).astype(BF16)
        k_hat = kf * e[_GLA_ROWS_BL:_GLA_ROWS_BL + CHUNK]
        ke_t = jnp.concatenate([k_hat, e[_GLA_ROWS_ONE:_GLA_ROWS_ONE + CHUNK]], axis=0).T
        return p.astype(BF16), q_hat, ke_t

    def output_and_state(h, p, q_hat, ke_t):
        vs = slice(h * dv, (h + 1) * dv)
        v = v_ref[:, vs]
        s_old = s_ref[h]
        o_ref[:, vs] = _dot(p, v) + _dot(q_hat, s_old.astype(BF16))
        decay = jnp.concatenate([ke_t[:, CHUNK:], ke_t[:, CHUNK:]], axis=1)
        v_pad = jnp.concatenate([v, jnp.zeros_like(v)], axis=0)
        s_ref[h] = s_old * jnp.concatenate([decay] * reps, axis=1) + _dot(ke_t.astype(BF16), v_pad)

    e_of, intra_of = {}, {}
    for step in range(n_heads + 2):
        if step < n_heads:
            e_of[step] = exponents(step)
        if 0 <= step - 1 < n_heads:
            intra_of[step - 1] = intra(step - 1, e_of.pop(step - 1))
        if 0 <= step - 2 < n_heads:
            output_and_state(step - 2, *intra_of.pop(step - 2))


def _gla_scan(z, g_low, wup_ext, b_g, *, n_batch, seq, ctx_len, dk, dv):
    t = z.shape[0]
    qk, d = H_A * dk, H_A * dv
    assert 2 * qk == d
    ncc, nlc = ctx_len // CHUNK, seq // CHUNK
    a_np, mask_np = _gla_constants()
    a_c = jnp.asarray(a_np, BF16)
    mask_c = jnp.asarray(mask_np, F32)
    rb = functools.partial(_chunk_row_block, n_ctx_chunks=ncc, n_lat_chunks=nlc, n_batch=n_batch)
    return pl.pallas_call(
        functools.partial(_gla_kernel, q_scale=dk ** -0.5, n_heads=H_A, dk=dk, dv=dv),
        out_shape=jax.ShapeDtypeStruct((2, t, d), F32),
        grid=(n_batch, 2, ncc + nlc),
        in_specs=[pl.BlockSpec((CHUNK, qk), lambda b, dr, c: (rb(b, dr, c), 0)),
                  pl.BlockSpec((CHUNK, qk), lambda b, dr, c: (rb(b, dr, c), 1)),
                  pl.BlockSpec((CHUNK, d), lambda b, dr, c: (rb(b, dr, c), 1)),
                  pl.BlockSpec((CHUNK, LANES), lambda b, dr, c: (rb(b, dr, c), 0)),
                  pl.BlockSpec((None, LANES, qk), lambda b, dr, c: (dr, 0, 0)),
                  pl.BlockSpec((None, 1, qk), lambda b, dr, c: (dr, 0, 0)),
                  pl.BlockSpec((None, _GLA_A_ROWS, CHUNK), lambda b, dr, c: (dr, 0, 0)),
                  pl.BlockSpec((None, len(_GLA_LEVELS) + 1, CHUNK, CHUNK), lambda b, dr, c: (dr, 0, 0, 0))],
        out_specs=pl.BlockSpec((None, CHUNK, d), lambda b, dr, c: (dr, rb(b, dr, c), 0)),
        scratch_shapes=[pltpu.VMEM((H_A, dk, dv), F32)],
        compiler_params=_cparams("arbitrary", "arbitrary", "arbitrary"),
        name="gla_scan",
    )(z, z, z, g_low, wup_ext, b_g, a_c, mask_c)


def _mlstm_kernel(q_ref, k_ref, v_ref, gates_ref, o_ref, cn_ref, m_ref, *, q_scale, n_heads, dk, dv):
    dr = pl.program_id(1)
    c = pl.program_id(2)

    @pl.when(c == 0)
    def _():
        cn_ref[...] = jnp.zeros(cn_ref.shape, F32)
        m_ref[...] = jnp.full(m_ref.shape, M_INIT, F32)

    ii = lax.broadcasted_iota(jnp.int32, (CHUNK, CHUNK), 0)
    jj = lax.broadcasted_iota(jnp.int32, (CHUNK, CHUNK), 1)
    eye = ii == jj
    causal = (jj - ii) * (1 - 2 * dr) <= 0
    lane = lax.broadcasted_iota(jnp.int32, (CHUNK, LANES), 1)
    ones_col = jnp.where(lane == 0, 1.0, 0.0).astype(BF16)

    def to_col(r):
        return jnp.sum(jnp.where(eye, jnp.broadcast_to(r, (CHUNK, CHUNK)), 0.0), axis=1, keepdims=True)

    def to_row(cl):
        return jnp.sum(jnp.where(eye, jnp.broadcast_to(cl, (CHUNK, CHUNK)), 0.0), axis=0, keepdims=True)

    def gate_weights(h):
        ic = GATE_SOFTCAP * jnp.tanh(gates_ref[h] * (1.0 / GATE_SOFTCAP))
        fc = _log_sigmoid(GATE_SOFTCAP * jnp.tanh(gates_ref[n_heads + h] * (1.0 / GATE_SOFTCAP)))
        b_col = jnp.sum(jnp.where(causal, jnp.broadcast_to(fc, (CHUNK, CHUNK)), 0.0), axis=1, keepdims=True)
        b_row = to_row(b_col)
        m_old = m_ref[h][0:1, 0:1]
        d_log = jnp.where(causal, b_col - b_row + ic, NEG_INF)
        inter_log = b_col + m_old
        m_row = jnp.maximum(inter_log, jnp.max(d_log, axis=1, keepdims=True))
        w_intra = jnp.exp(d_log - m_row)
        w_inter = jnp.exp(inter_log - m_row)
        floor = jnp.exp(-m_row)
        b_last = jnp.sum(fc, axis=1, keepdims=True)
        w_log = b_last - b_row + ic
        m_new = jnp.maximum(b_last + m_old, jnp.max(w_log, axis=1, keepdims=True))
        keep = jnp.exp(b_last + m_old - m_new)
        w_col = to_col(jnp.exp(w_log - m_new))
        m_ref[h] = jnp.broadcast_to(m_new, m_ref.shape[1:])
        return w_intra, w_inter, floor, keep, w_col

    def output_and_state(h, w_intra, w_inter, floor, keep, w_col):
        ks = slice(h * dk, (h + 1) * dk)
        vs = slice(h * dv, (h + 1) * dv)
        q = (q_ref[:, ks].astype(F32) * q_scale).astype(BF16)
        k = k_ref[:, ks]
        v_ext = jnp.concatenate([v_ref[:, vs], ones_col], axis=1)
        s = _dot_nt(q, k) * w_intra
        cn_old = cn_ref[h]
        nd = _dot(s.astype(BF16), v_ext) + w_inter * _dot(q, cn_old.astype(BF16))
        den = nd[:, dv:dv + 1]
        o_ref[:, vs] = nd[:, :dv] / jnp.maximum(jnp.abs(den), floor)
        wk = k.astype(F32) * w_col
        wk_t = jnp.concatenate([wk, jnp.zeros_like(wk)], axis=0).T
        v_pad = jnp.concatenate([v_ext, jnp.zeros_like(v_ext)], axis=0)
        cn_ref[h] = keep * cn_old + _dot(wk_t.astype(BF16), v_pad)

    weights = {}
    for step in range(n_heads + 1):
        if step < n_heads:
            weights[step] = gate_weights(step)
        if step >= 1:
            output_and_state(step - 1, *weights.pop(step - 1))


def _mlstm_scan(z, gates, *, n_batch, seq, ctx_len, dk, dv):
    t = z.shape[0]
    qk, d = H_B * dk, H_B * dv
    assert 2 * qk == d
    ncc, nlc = ctx_len // CHUNK, seq // CHUNK
    rb = functools.partial(_chunk_row_block, n_ctx_chunks=ncc, n_lat_chunks=nlc, n_batch=n_batch)
    return pl.pallas_call(
        functools.partial(_mlstm_kernel, q_scale=dk ** -0.5, n_heads=H_B, dk=dk, dv=dv),
        out_shape=jax.ShapeDtypeStruct((2, t, d), F32),
        grid=(n_batch, 2, ncc + nlc),
        in_specs=[pl.BlockSpec((CHUNK, qk), lambda b, dr, c: (rb(b, dr, c), 0)),
                  pl.BlockSpec((CHUNK, qk), lambda b, dr, c: (rb(b, dr, c), 1)),
                  pl.BlockSpec((CHUNK, d), lambda b, dr, c: (rb(b, dr, c), 1)),
                  pl.BlockSpec((None, None, 2 * H_B, 1, CHUNK), lambda b, dr, c: (dr, rb(b, dr, c), 0, 0, 0))],
        out_specs=pl.BlockSpec((None, CHUNK, d), lambda b, dr, c: (dr, rb(b, dr, c), 0)),
        scratch_shapes=[pltpu.VMEM((H_B, dk, dv + LANES), F32), pltpu.VMEM((H_B, 8, LANES), F32)],
        compiler_params=_cparams("arbitrary", "arbitrary", "arbitrary"),
        name="mlstm_scan",
    )(z, z, z, gates)


def _readout_kernel(o_ref, g_ref, ng_ref, y_ref, *, n_heads, gate):
    o = o_ref[0] + o_ref[1]
    g = g_ref[...].astype(F32)
    act = g * _sigmoid(g) if gate == "silu" else _sigmoid(g)
    dh = o.shape[1] // n_heads
    for hh in range(n_heads):
        sl = slice(hh * dh, (hh + 1) * dh)
        oh = o[:, sl]
        r = lax.rsqrt(jnp.mean(oh * oh, axis=-1, keepdims=True) + RMS_EPS)
        y_ref[:, sl] = (oh * r * ng_ref[:, sl] * act[:, sl]).astype(y_ref.dtype)


def _readout(o2, z, norm_g, *, n_heads, gate, gate_col0, rows):
    d = o2.shape[2]
    tr = _pick_tile(rows, 128, 8)
    gblk = gate_col0 // d
    return pl.pallas_call(
        functools.partial(_readout_kernel, n_heads=n_heads, gate=gate),
        out_shape=jax.ShapeDtypeStruct((rows, d), BF16),
        grid=(rows // tr,),
        in_specs=[pl.BlockSpec((2, tr, d), lambda i: (0, i, 0)),
                  pl.BlockSpec((tr, d), lambda i: (i, gblk)),
                  pl.BlockSpec((1, d), lambda i: (0, 0))],
        out_specs=pl.BlockSpec((tr, d), lambda i: (i, 0)),
        compiler_params=_cparams("parallel"),
        name="readout",
    )(o2, z, norm_g.reshape(1, d))


def _flash_kernel(*refs, n_seg):
    q_ref = refs[0]
    kv = refs[1:1 + 2 * n_seg]
    o_ref = refs[1 + 2 * n_seg]
    vx_refs = refs[2 + 2 * n_seg:]
    q = q_ref[...]
    tq = q.shape[0]
    m = jnp.full((1, tq), NEG_INF, F32)
    acc = jnp.zeros((D_V_C + ONES_ROWS, tq), F32)

    @pl.when(pl.program_id(2) == 0)
    def _():
        for sg in range(n_seg):
            vx_refs[sg][:D_V_C, :] = kv[2 * sg + 1][...]
            vx_refs[sg][D_V_C:, :] = jnp.ones((ONES_ROWS, vx_refs[sg].shape[1]), BF16)

    blocks = []
    for sg in range(n_seg):
        k_ref, vt_ref = kv[2 * sg], vx_refs[sg]
        length = k_ref.shape[0]
        tk = min(KV_TILE, length)
        blocks += [(k_ref, vt_ref, t * tk, tk) for t in range(length // tk)]

    def scores(blk):
        k_ref, _, lo, tk = blk
        return _dot_nt(k_ref[lo:lo + tk, :], q)

    st_next = scores(blocks[0])
    for t, (_, vt_ref, lo, tk) in enumerate(blocks):
        st = st_next
        if t + 1 < len(blocks):
            st_next = scores(blocks[t + 1])
        m_new = jnp.maximum(m, jnp.max(st, axis=0, keepdims=True))
        alpha = jnp.exp2(m - m_new)
        p = jnp.exp2(st - m_new)
        acc = alpha * acc + _dot(vt_ref[:, lo:lo + tk], p.astype(BF16))
        m = m_new
    o_ref[...] = (acc[:D_V_C] / acc[D_V_C:D_V_C + 1]).T.astype(o_ref.dtype)


def _flash(qcat, kcat, vt, *, n_batch, q_rows0, q_len, segs):
    tq = _pick_tile(q_len, 512, 8)
    nq = q_len // tq
    in_specs = [pl.BlockSpec((tq, MXU_DIM), lambda b, h, i: ((q_rows0 + b * q_len) // tq + i, h))]
    args = [qcat]
    for row0, length in segs:
        in_specs.append(pl.BlockSpec((length, MXU_DIM), lambda b, h, i, row0=row0, length=length: (row0 // length + b, h)))
        in_specs.append(pl.BlockSpec((D_V_C, length), lambda b, h, i, row0=row0, length=length: (h, row0 // length + b)))
        args += [kcat, vt]
    return pl.pallas_call(
        functools.partial(_flash_kernel, n_seg=len(segs)),
        out_shape=jax.ShapeDtypeStruct((n_batch * q_len, H_C * D_V_C), BF16),
        grid=(n_batch, H_C, nq),
        in_specs=in_specs,
        out_specs=pl.BlockSpec((tq, D_V_C), lambda b, h, i: (b * nq + i, h)),
        scratch_shapes=[pltpu.VMEM((D_V_C + ONES_ROWS, length), BF16) for _, length in segs],
        compiler_params=_cparams("parallel", "parallel", "arbitrary"),
        name="flash",
    )(*args)


def _mla_prep_kernel(z_ref, gq_ref, gkv_ref, cos_ref, sin_ref, cq_ref, akv_ref, *, q_rank, kv_rank):
    cq = z_ref[:, :q_rank]
    cq_ref[...] = (cq * lax.rsqrt(jnp.mean(cq * cq, axis=-1, keepdims=True) + RMS_EPS) * gq_ref[...]).astype(BF16)
    ckv = z_ref[:, q_rank:q_rank + kv_rank]
    akv_ref[:, :kv_rank] = (ckv * lax.rsqrt(jnp.mean(ckv * ckv, axis=-1, keepdims=True) + RMS_EPS)
                            * gkv_ref[...]).astype(BF16)
    x = z_ref[:, q_rank + kv_rank:q_rank + kv_rank + LANES]
    lane = lax.broadcasted_iota(jnp.int32, x.shape, 1)
    swapped = jnp.where(lane < D_ROPE // 2, pltpu.roll(x, LANES - D_ROPE // 2, 1), pltpu.roll(x, D_ROPE // 2, 1))
    akv_ref[:, kv_rank:] = (x * cos_ref[...] + swapped * sin_ref[...]).astype(BF16)


def _mla_prep(zd, gq, gkv, cos, sin, *, q_rank, kv_rank):
    t = zd.shape[0]
    tr = _pick_tile(t, 256, 8)
    return pl.pallas_call(
        functools.partial(_mla_prep_kernel, q_rank=q_rank, kv_rank=kv_rank),
        out_shape=(jax.ShapeDtypeStruct((t, q_rank), BF16), jax.ShapeDtypeStruct((t, kv_rank + LANES), BF16)),
        grid=(t // tr,),
        in_specs=[pl.BlockSpec((tr, zd.shape[1]), lambda i: (i, 0)),
                  pl.BlockSpec((1, q_rank), lambda i: (0, 0)),
                  pl.BlockSpec((1, kv_rank), lambda i: (0, 0)),
                  pl.BlockSpec((tr, LANES), lambda i: (i, 0)),
                  pl.BlockSpec((tr, LANES), lambda i: (i, 0))],
        out_specs=(pl.BlockSpec((tr, q_rank), lambda i: (i, 0)),
                   pl.BlockSpec((tr, kv_rank + LANES), lambda i: (i, 0))),
        compiler_params=_cparams("parallel"),
        name="mla_prep",
    )(zd, gq.reshape(1, q_rank), gkv.reshape(1, kv_rank), cos, sin)


def _rope_tables(n_batch, seq, ctx_len):
    n_rows = seq // GRID_W
    rows = np.repeat(np.arange(n_rows, dtype=np.float32), GRID_W)
    cols = np.tile(np.arange(GRID_W, dtype=np.float32), n_rows)
    half = D_ROPE // 2
    inv_freq = ROPE_THETA ** (-jnp.arange(0, half, 2, dtype=F32) / half)
    ang = jnp.concatenate([jnp.asarray(rows)[:, None] * inv_freq, jnp.asarray(cols)[:, None] * inv_freq], axis=-1)
    cos, sin = jnp.cos(ang), jnp.sin(ang)
    pad1 = jnp.ones((seq, LANES - D_ROPE), F32)
    pad0 = jnp.zeros((seq, LANES - D_ROPE), F32)
    cos_l = jnp.concatenate([cos, cos, pad1], axis=1)
    sin_l = jnp.concatenate([-sin, sin, pad0], axis=1)
    cos_t = jnp.concatenate([jnp.tile(cos_l, (n_batch, 1)), jnp.ones((n_batch * ctx_len, LANES), F32)], axis=0)
    sin_t = jnp.concatenate([jnp.tile(sin_l, (n_batch, 1)), jnp.zeros((n_batch * ctx_len, LANES), F32)], axis=0)
    return cos_t, sin_t


def _gla_mixer(h, w_in_b, j, w_gdown, w_gup, b_g, norm_g, dims):
    n_batch, seq, ctx_len, d = dims
    dk, dv = d // (2 * H_A), d // H_A
    qk = H_A * dk
    z = _mm(h, w_in_b, w_lead=(j,), out_dtype=BF16)
    w_low = jnp.concatenate([w_gdown[0], w_gdown[1],
                             jnp.zeros((d, LANES - 2 * GATE_RANK_A), F32)], axis=1).astype(BF16)
    g_low = _mm(h, w_low, out_dtype=BF16)
    wup = jnp.zeros((2, LANES, qk), F32)
    wup = wup.at[0, :GATE_RANK_A].set(w_gup[0]).at[1, GATE_RANK_A:2 * GATE_RANK_A].set(w_gup[1]).astype(BF16)
    o2 = _gla_scan(z, g_low, wup, b_g.reshape(2, 1, qk), n_batch=n_batch, seq=seq, ctx_len=ctx_len, dk=dk, dv=dv)
    return _readout(o2, z, norm_g, n_heads=H_A, gate="silu", gate_col0=2 * qk + d, rows=z.shape[0])


def _mlstm_mixer(h, w_in_b, j, w_if, b_if, norm_g, dims):
    n_batch, seq, ctx_len, d = dims
    dk, dv = d // (2 * H_B), d // H_B
    qk = H_B * dk
    t = h.shape[0]
    z = _mm(h, w_in_b, w_lead=(j,), out_dtype=BF16)
    w_g = jnp.concatenate([w_if[0], w_if[1], jnp.zeros((d, LANES - 4 * H_B), F32)], axis=1).astype(BF16)
    pre = _mm(h, w_g, out_dtype=F32)[:, :4 * H_B] + jnp.concatenate([b_if[0], b_if[1]])[None, :]
    gates = pre.reshape(t // CHUNK, CHUNK, 2, 2 * H_B).transpose(2, 0, 3, 1)[:, :, :, None, :]
    o2 = _mlstm_scan(z, gates, n_batch=n_batch, seq=seq, ctx_len=ctx_len, dk=dk, dv=dv)
    return _readout(o2, z, norm_g, n_heads=H_B, gate="sigmoid", gate_col0=2 * qk + d, rows=t)


def _mla_mixer(h, w_down, q_norm_g, w_uq, kv_norm_g, w_ukv, dims, need_ctx_out):
    n_batch, seq, ctx_len, d = dims
    q_rank, kv_rank = d // 4, d // 8
    n_lat = n_batch * seq
    scale = (D_NOPE + D_ROPE) ** -0.5 * LOG2_E
    cos_t, sin_t = _rope_tables(n_batch, seq, ctx_len)

    n_down = q_rank + kv_rank + LANES
    w_down_ext = jnp.concatenate([w_down, jnp.zeros((d, n_down - w_down.shape[1]), F32)], axis=1).astype(BF16)
    zd = _mm(h, w_down_ext, out_dtype=F32, tn_target=2048)
    cqn, akv = _mla_prep(zd, q_norm_g, kv_norm_g, cos_t, sin_t, q_rank=q_rank, kv_rank=kv_rank)

    wq = w_uq.reshape(q_rank, H_C, D_NOPE + D_ROPE)
    wq = jnp.concatenate([wq, jnp.zeros((q_rank, H_C, MXU_DIM - D_NOPE - D_ROPE), F32)], axis=2)
    qcat = _mm(cqn, wq.reshape(q_rank, H_C * MXU_DIM).astype(BF16), out_dtype=BF16,
               epilogue="rope", cos=cos_t, sin=sin_t, scale=scale)

    wkv = w_ukv.reshape(kv_rank, H_C, D_NOPE + D_V_C)
    wk_top = jnp.concatenate([wkv[:, :, :D_NOPE], jnp.zeros((kv_rank, H_C, MXU_DIM - D_NOPE), F32)], axis=2)
    ident = jnp.concatenate([jnp.zeros((LANES, D_NOPE), F32), jnp.eye(LANES, dtype=F32)], axis=1)
    wk_ext = jnp.concatenate([wk_top, jnp.broadcast_to(ident[:, None, :], (LANES, H_C, MXU_DIM))], axis=0)
    kcat = _mm(akv, wk_ext.reshape(kv_rank + LANES, H_C * MXU_DIM).astype(BF16), out_dtype=BF16)
    wv_t = jnp.concatenate([wkv[:, :, D_NOPE:].reshape(kv_rank, H_C * D_V_C).T,
                            jnp.zeros((H_C * D_V_C, LANES), F32)], axis=1)
    vt = _mm(wv_t.astype(BF16), akv.T, out_dtype=BF16)

    o_lat = _flash(qcat, kcat, vt, n_batch=n_batch, q_rows0=0, q_len=seq, segs=[(0, seq), (n_lat, ctx_len)])
    if not need_ctx_out:
        return o_lat
    o_ctx = _flash(qcat, kcat, vt, n_batch=n_batch, q_rows0=n_lat, q_len=ctx_len, segs=[(n_lat, ctx_len)])
    return jnp.concatenate([o_lat, o_ctx], axis=0)


def _modulations(cond, ada_down_b, ada_up_b, ada_bias, i):
    d = cond.shape[1]
    low = _mm(cond, ada_down_b, w_lead=(i,), out_dtype=BF16, prologue="silu")
    m = _mm(low, ada_up_b, w_lead=(i,), out_dtype=F32) + ada_bias[i][None, :]
    return m.reshape(cond.shape[0], N_MOD, d)


def kernel(x, c, ctx, c_ctx, ada_down, ada_up, ada_bias, ln_g, ln_b, ffn_w_in, ffn_w_out, gla_w_in, gla_w_gdown, gla_w_gup, gla_b_g, gla_norm_g, gla_w_out, mlstm_w_in, mlstm_w_if, mlstm_b_if, mlstm_norm_g, mlstm_w_out, mla_w_down, mla_q_norm_g, mla_w_uq, mla_kv_norm_g, mla_w_ukv, mla_w_out):
    n_batch, seq, d = x.shape
    ctx_len = ctx.shape[1]
    depth = ada_down.shape[0]
    n_lat = n_batch * seq
    n_ctx = n_batch * ctx_len
    t_all = n_lat + n_ctx
    dims = (n_batch, seq, ctx_len, d)
    alpha = (2.0 * depth) ** 0.25
    tm = math.gcd(math.gcd(seq, n_ctx), ROW_TILE_TARGET)
    n_seg = n_batch + 1

    def seg_of_tile(i):
        return jnp.minimum((i * tm) // seq, n_batch)

    xs = jnp.concatenate([x.reshape(n_lat, d), ctx.reshape(n_ctx, d)], axis=0)
    cond = jnp.concatenate([c, c_ctx[None, :], jnp.zeros((8 - n_seg, d), F32)], axis=0)

    ada_down_b, ada_up_b = ada_down.astype(BF16), ada_up.astype(BF16)
    ffn_w_in_b, ffn_w_out_b = ffn_w_in.astype(BF16), ffn_w_out.astype(BF16)
    gla_w_in_b, gla_w_out_b = gla_w_in.astype(BF16), gla_w_out.astype(BF16)
    mlstm_w_in_b, mlstm_w_out_b = mlstm_w_in.astype(BF16), mlstm_w_out.astype(BF16)
    mla_w_out_b = mla_w_out.astype(BF16)

    def mod_vecs(m, s, weight):
        shift = m[:n_seg, 3 * s][:, None, :]
        scale1p = 1.0 + m[:n_seg, 3 * s + 1][:, None, :]
        gate_w = weight * m[:n_seg, 3 * s + 2][:, None, :]
        return shift, scale1p, gate_w

    mods = [_modulations(cond, ada_down_b, ada_up_b, ada_bias, i) for i in range(depth)]
    shift0, scale0, _ = mod_vecs(mods[0], 0, MACARON_WEIGHT)
    hcur = _modulate(xs, scale0, shift0, seg_of_tile=seg_of_tile, tm=tm)

    for i in range(depth):
        last = i == depth - 1
        m = mods[i]
        _, _, gw0 = mod_vecs(m, 0, MACARON_WEIGHT)
        sh1, sc1, gw1 = mod_vecs(m, 1, 1.0)
        sh2, sc2, gw2 = mod_vecs(m, 2, MACARON_WEIGHT)
        pln = functools.partial(_proj_ln, seg_of_tile=seg_of_tile, tm=tm, alpha=alpha)

        act = _ffn1(hcur, ffn_w_in_b, t_all, w_lead=(i, 0))
        xs, hcur = pln(act, ffn_w_out_b, xs, gw0, ln_g[i, 0], ln_b[i, 0], sc1, sh1, rows=t_all, w_lead=(i, 0))

        kind, j = i % 3, i // 3
        rows1 = n_lat if last else t_all
        if kind == 0:
            y_in = _gla_mixer(hcur, gla_w_in_b, j, gla_w_gdown[j], gla_w_gup[j], gla_b_g[j], gla_norm_g[j], dims)
            w_o = gla_w_out_b
        elif kind == 1:
            y_in = _mlstm_mixer(hcur, mlstm_w_in_b, j, mlstm_w_if[j], mlstm_b_if[j], mlstm_norm_g[j], dims)
            w_o = mlstm_w_out_b
        else:
            y_in = _mla_mixer(hcur, mla_w_down[j], mla_q_norm_g[j], mla_w_uq[j], mla_kv_norm_g[j], mla_w_ukv[j],
                              dims, not last)
            w_o = mla_w_out_b
        xs, hcur = pln(y_in, w_o, xs, gw1, ln_g[i, 1], ln_b[i, 1], sc2, sh2, rows=rows1, w_lead=(j,))

        if last:
            sh_n, sc_n = sh2, sc2
        else:
            sh_n, sc_n, _ = mod_vecs(mods[i + 1], 0, MACARON_WEIGHT)
        act = _ffn1(hcur, ffn_w_in_b, rows1, w_lead=(i, 1))
        xs, hcur = pln(act, ffn_w_out_b, xs, gw2, ln_g[i, 2], ln_b[i, 2], sc_n, sh_n, rows=rows1, w_lead=(i, 1))

    return xs.reshape(n_batch, seq, d)
```

```python
import functools
import math

import numpy as np
import jax
import jax.numpy as jnp
from jax import lax
from jax.experimental import pallas as pl
from jax.experimental.pallas import tpu as pltpu

F32 = jnp.float32
BF16 = jnp.bfloat16

N_MOD = 9
MACARON_WEIGHT = 0.5
LN_EPS = 1e-5
RMS_EPS = 1e-6
CHUNK = 64
H_A = 8
GATE_RANK_A = 16
GATE_NORMALIZER_A = 16.0
H_B = 8
GATE_SOFTCAP = 15.0
M_INIT = -1e30
H_C = 32
D_NOPE = 128
D_ROPE = 64
D_V_C = 128
ROPE_THETA = 10000.0
GRID_W = 64

LANES = 128
MXU_DIM = 256
VMEM_LIMIT_BYTES = 60 * 1024 * 1024
ROW_TILE_TARGET = 512
COL_TILE_TARGET = 2048
MM_VMEM_BUDGET = 48 * 1024 * 1024
ELEMENTWISE_TILE_ROWS = 256
LN_ROWS = 32
KV_TILE = 512
ONES_ROWS = 16

NEG_INF = float("-inf")
LOG2_E = math.log2(math.e)


def _cparams(*sem):
    return pltpu.CompilerParams(dimension_semantics=sem, vmem_limit_bytes=VMEM_LIMIT_BYTES)


def _pick_tile(n, target, mult=LANES):
    if n <= target:
        return n
    t = (target // mult) * mult
    while t > mult and n % t:
        t -= mult
    assert n % t == 0, (n, target, mult)
    return t


def _sigmoid(x):
    return 1.0 / (1.0 + jnp.exp(-x))


def _log_sigmoid(x):
    return jnp.minimum(x, 0.0) - jnp.log1p(jnp.exp(-jnp.abs(x)))


def _dot(a, b):
    return jnp.dot(a, b, preferred_element_type=F32)


def _dot_nt(a, b):
    return lax.dot_general(a, b, (((1,), (1,)), ((), ())), preferred_element_type=F32)


def _split2(x):
    hi = x.astype(BF16)
    lo = (x - hi.astype(F32)).astype(BF16)
    return hi, lo


def _wspec(lead, block, index_map):
    return pl.BlockSpec((None,) * len(lead) + block, lambda *g: tuple(lead) + index_map(*g))


def _resident_weights(w_ref, wb_ref):
    @pl.when(pl.program_id(1) == 0)
    def _():
        def body(r, carry):
            rows = pl.ds(pl.multiple_of(r * MXU_DIM, MXU_DIM), MXU_DIM)
            wb_ref[rows, :] = w_ref[rows, :].astype(wb_ref.dtype)
            return carry
        lax.fori_loop(0, w_ref.shape[0] // MXU_DIM, body, 0)
    return wb_ref


def _mm_kernel(*refs, prologue, epilogue, scale, cast_w):
    a_ref, w_ref = refs[0], refs[1]
    if cast_w:
        w_ref = _resident_weights(w_ref, refs[-1])
        refs = refs[:-1]
    o_ref = refs[-1]
    extra = refs[2:-1]
    a = a_ref[...]
    if prologue == "silu":
        af = a.astype(F32)
        a = (af * _sigmoid(af)).astype(BF16)
    acc = _dot(a, w_ref[...])
    if epilogue == "rope":
        cos = extra[0][...]
        sin = extra[1][...]
        lane = lax.broadcasted_iota(jnp.int32, cos.shape, 1)
        first_half = lane < (D_ROPE // 2)
        for g in range(acc.shape[1] // MXU_DIM):
            lo = g * MXU_DIM
            o_ref[:, lo:lo + LANES] = (acc[:, lo:lo + LANES] * scale).astype(o_ref.dtype)
            x = acc[:, lo + LANES:lo + 2 * LANES]
            swapped = jnp.where(first_half, pltpu.roll(x, LANES - D_ROPE // 2, 1), pltpu.roll(x, D_ROPE // 2, 1))
            o_ref[:, lo + LANES:lo + 2 * LANES] = ((x * cos + swapped * sin) * scale).astype(o_ref.dtype)
    else:
        o_ref[...] = acc.astype(o_ref.dtype)


def _mm_tiles(m, kk, n, a_bytes, w_bytes, out_bytes, cast_w, tn_mult):
    tm = _pick_tile(m, ROW_TILE_TARGET, 8)
    tn_target = COL_TILE_TARGET
    while True:
        tn = _pick_tile(n, tn_target, tn_mult)
        need = 2 * tm * kk * a_bytes + 2 * kk * tn * w_bytes + 2 * tm * tn * out_bytes + tm * tn * 4
        need += kk * tn * 2 if cast_w else 0
        if need <= MM_VMEM_BUDGET or tn <= tn_mult:
            return tm, tn
        tn_target = tn - tn_mult


def _mm(a, w, *, out_dtype, w_lead=(), rows=None, prologue=None, epilogue=None, cos=None, sin=None, scale=1.0):
    m = a.shape[0] if rows is None else rows
    kk, n = w.shape[-2:]
    assert a.shape[1] == kk or kk % LANES == 0
    cast_w = w.dtype != BF16
    tm, tn = _mm_tiles(m, kk, n, a.dtype.itemsize, w.dtype.itemsize, jnp.dtype(out_dtype).itemsize, cast_w,
                       MXU_DIM if epilogue == "rope" else LANES)
    in_specs = [pl.BlockSpec((tm, kk), lambda j, i: (i, 0)),
                _wspec(w_lead, (kk, tn), lambda j, i: (0, j))]
    args = [a, w]
    if epilogue == "rope":
        in_specs += [pl.BlockSpec((tm, LANES), lambda j, i: (i, 0))] * 2
        args += [cos, sin]
    return pl.pallas_call(
        functools.partial(_mm_kernel, prologue=prologue, epilogue=epilogue, scale=scale, cast_w=cast_w),
        out_shape=jax.ShapeDtypeStruct((m, n), out_dtype),
        grid=(n // tn, m // tm),
        in_specs=in_specs,
        out_specs=pl.BlockSpec((tm, tn), lambda j, i: (i, j)),
        scratch_shapes=[pltpu.VMEM((kk, tn), BF16)] if cast_w else [],
        compiler_params=_cparams("parallel", "arbitrary"),
        name="mm_" + (prologue or "p") + "_" + (epilogue or "e"),
    )(*args)


def _ffn1_kernel(h_ref, wa_ref, wu_ref, o_ref, *scratch):
    if scratch:
        wa_ref = _resident_weights(wa_ref, scratch[0])
        wu_ref = _resident_weights(wu_ref, scratch[1])
    h = h_ref[...]
    a = _dot(h, wa_ref[...])
    u = _dot(h, wu_ref[...])
    o_ref[...] = (a * _sigmoid(a) * u).astype(o_ref.dtype)


def _ffn1(h, w_in, rows, w_lead=()):
    d, f2 = w_in.shape[-2:]
    f = f2 // 2
    cast_w = w_in.dtype != BF16
    tm = _pick_tile(rows, ROW_TILE_TARGET, 8)
    tn = _pick_tile(f, 512)
    nj = f // tn
    return pl.pallas_call(
        _ffn1_kernel,
        out_shape=jax.ShapeDtypeStruct((rows, f), BF16),
        grid=(nj, rows // tm),
        in_specs=[pl.BlockSpec((tm, d), lambda j, i: (i, 0)),
                  _wspec(w_lead, (d, tn), lambda j, i: (0, j)),
                  _wspec(w_lead, (d, tn), lambda j, i: (0, j + nj))],
        out_specs=pl.BlockSpec((tm, tn), lambda j, i: (i, j)),
        scratch_shapes=[pltpu.VMEM((d, tn), BF16)] * 2 if cast_w else [],
        compiler_params=_cparams("parallel", "arbitrary"),
        name="ffn1",
    )(h, w_in, w_in)


def _res_ln_kernel(y_ref, x_ref, gw_ref, lng_ref, lnb_ref, sc_ref, sh_ref, xo_ref, ho_ref, *, alpha):
    gw = gw_ref[...]
    g = lng_ref[...]
    b = lnb_ref[...]
    sc = sc_ref[...]
    sh = sh_ref[...]

    def slab(r, carry):
        rows = pl.ds(pl.multiple_of(r * LN_ROWS, LN_ROWS), LN_ROWS)
        v = alpha * x_ref[rows, :] + gw * y_ref[rows, :]
        mu = jnp.mean(v, axis=-1, keepdims=True)
        vc = v - mu
        var = jnp.mean(vc * vc, axis=-1, keepdims=True)
        xn = vc * lax.rsqrt(var + LN_EPS) * g + b
        xo_ref[rows, :] = xn
        ho_ref[rows, :] = (xn * sc + sh).astype(ho_ref.dtype)
        return carry

    lax.fori_loop(0, xo_ref.shape[0] // LN_ROWS, slab, 0)


def _res_ln(y, x, gate_w, ln_g, ln_b, scale1p, shift, *, rows, seg_of_row, alpha):
    d = y.shape[1]
    tr = math.gcd(rows, ELEMENTWISE_TILE_ROWS)
    vec = pl.BlockSpec((None, 1, d), lambda i: (seg_of_row(i * tr), 0, 0))
    cvec = pl.BlockSpec((1, d), lambda i: (0, 0))
    tile = pl.BlockSpec((tr, d), lambda i: (i, 0))
    return pl.pallas_call(
        functools.partial(_res_ln_kernel, alpha=alpha),
        out_shape=(jax.ShapeDtypeStruct((rows, d), F32), jax.ShapeDtypeStruct((rows, d), BF16)),
        grid=(rows // tr,),
        in_specs=[tile, tile, vec, cvec, cvec, vec, vec],
        out_specs=(tile, tile),
        compiler_params=_cparams("parallel"),
        name="res_ln",
    )(y, x, gate_w, ln_g.reshape(1, d), ln_b.reshape(1, d), scale1p, shift)


def _modulate_kernel(x_ref, sc_ref, sh_ref, o_ref):
    o_ref[...] = (x_ref[...] * sc_ref[...] + sh_ref[...]).astype(o_ref.dtype)


def _modulate(x, scale1p, shift, *, seg_of_row):
    t, d = x.shape
    tr = math.gcd(t, ELEMENTWISE_TILE_ROWS)
    vec = pl.BlockSpec((None, 1, d), lambda i: (seg_of_row(i * tr), 0, 0))
    return pl.pallas_call(
        _modulate_kernel,
        out_shape=jax.ShapeDtypeStruct((t, d), BF16),
        grid=(t // tr,),
        in_specs=[pl.BlockSpec((tr, d), lambda i: (i, 0)), vec, vec],
        out_specs=pl.BlockSpec((tr, d), lambda i: (i, 0)),
        compiler_params=_cparams("parallel"),
        name="modulate",
    )(x, scale1p, shift)


def _chunk_row_block(b, dr, c, *, n_ctx_chunks, n_lat_chunks, n_batch):
    jc = jnp.where(dr == 1, n_ctx_chunks - 1 - c, c)
    jl = jnp.where(dr == 1, n_lat_chunks - 1 - (c - n_ctx_chunks), c - n_ctx_chunks)
    ctx_blk = n_batch * n_lat_chunks + b * n_ctx_chunks + jc
    lat_blk = b * n_lat_chunks + jl
    return jnp.where(c < n_ctx_chunks, ctx_blk, lat_blk)


_GLA_LEVELS = (32, 16, 8, 4, 2, 1)
_GLA_ROWS_B = len(_GLA_LEVELS) * CHUNK
_GLA_ROWS_BL = _GLA_ROWS_B + CHUNK
_GLA_ROWS_ONE = _GLA_ROWS_BL + CHUNK
_GLA_A_ROWS = _GLA_ROWS_ONE + CHUNK


def _gla_constants():
    c = CHUNK
    a = np.zeros((_GLA_A_ROWS, c), np.float32)
    masks = np.zeros((len(_GLA_LEVELS) + 1, c, c), np.float32)
    t = np.arange(c)
    for l, s in enumerate(_GLA_LEVELS):
        for i in range(c):
            r = (i // (2 * s)) * 2 * s + s - 1
            if i % (2 * s) >= s:
                a[l * c + i] = (t > r) & (t <= i)
                lo = (i // (2 * s)) * 2 * s
                masks[l, i, lo:lo + s] = 1.0
            else:
                a[l * c + i] = (t > i) & (t <= r)
    for i in range(c):
        a[_GLA_ROWS_B + i] = t <= i
        a[_GLA_ROWS_BL + i] = t > i
        a[_GLA_ROWS_ONE + i] = 1.0
    masks[len(_GLA_LEVELS)] = np.eye(c)
    a_rev = a.reshape(-1, c, c)[:, ::-1, ::-1].reshape(-1, c)
    masks_rev = masks[:, ::-1, ::-1]
    return (np.stack([a, a_rev]).astype(np.float32), np.stack([masks, masks_rev]).astype(np.float32))


def _gla_kernel(q_ref, k_ref, v_ref, g_ref, wup_ref, bg_ref, a_ref, mask_ref, o_ref, s_ref, *,
                q_scale, n_heads, dk, dv):
    dr = pl.program_id(1)
    c = pl.program_id(2)

    @pl.when(c == 0)
    def _():
        s_ref[...] = jnp.zeros(s_ref.shape, F32)

    pre = _dot(g_ref[...], wup_ref[...]) + bg_ref[...]
    la_all = _log_sigmoid(pre) * (1.0 / GATE_NORMALIZER_A)
    amat = a_ref[...]
    row = lax.broadcasted_iota(jnp.int32, (CHUNK, 1), 0)
    row = jnp.where(dr == 1, CHUNK - 1 - row, row)
    reps = dv // LANES

    def exponents(h):
        hi, lo = _split2(la_all[:, h * dk:(h + 1) * dk])
        return jnp.exp(_dot(amat, hi) + _dot(amat, lo))

    def intra(h, e):
        ks = slice(h * dk, (h + 1) * dk)
        qf = q_ref[:, ks].astype(F32) * q_scale
        kf = k_ref[:, ks].astype(F32)
        p = mask_ref[len(_GLA_LEVELS)] * _dot_nt(qf.astype(BF16), kf.astype(BF16))
        for l, s in enumerate(_GLA_LEVELS):
            upper = (row & s) != 0
            m = (jnp.where(upper, qf, kf) * e[l * CHUNK:(l + 1) * CHUNK]).astype(BF16)
            p = p + mask_ref[l] * _dot_nt(m, m)
        q_hat = (qf * e[_GLA_ROWS_B:_GLA_ROWS_B + CHUNK]).astype(BF16)
        k_hat = kf * e[_GLA_ROWS_BL:_GLA_ROWS_BL + CHUNK]
        ke_t = jnp.concatenate([k_hat, e[_GLA_ROWS_ONE:_GLA_ROWS_ONE + CHUNK]], axis=0).T
        return p.astype(BF16), q_hat, ke_t

    def output_and_state(h, p, q_hat, ke_t):
        vs = slice(h * dv, (h + 1) * dv)
        v = v_ref[:, vs]
        s_old = s_ref[h]
        o_ref[:, vs] = _dot(p, v) + _dot(q_hat, s_old.astype(BF16))
        decay = jnp.concatenate([ke_t[:, CHUNK:], ke_t[:, CHUNK:]], axis=1)
        v_pad = jnp.concatenate([v, jnp.zeros_like(v)], axis=0)
        s_ref[h] = s_old * jnp.concatenate([decay] * reps, axis=1) + _dot(ke_t.astype(BF16), v_pad)

    e_of, intra_of = {}, {}
    for step in range(n_heads + 2):
        if step < n_heads:
            e_of[step] = exponents(step)
        if 0 <= step - 1 < n_heads:
            intra_of[step - 1] = intra(step - 1, e_of.pop(step - 1))
        if 0 <= step - 2 < n_heads:
            output_and_state(step - 2, *intra_of.pop(step - 2))


def _gla_scan(z, g_low, wup_ext, b_g, *, n_batch, seq, ctx_len, dk, dv):
    t = z.shape[0]
    qk, d = H_A * dk, H_A * dv
    assert 2 * qk == d
    ncc, nlc = ctx_len // CHUNK, seq // CHUNK
    a_np, mask_np = _gla_constants()
    a_c = jnp.asarray(a_np, BF16)
    mask_c = jnp.asarray(mask_np, F32)
    rb = functools.partial(_chunk_row_block, n_ctx_chunks=ncc, n_lat_chunks=nlc, n_batch=n_batch)
    return pl.pallas_call(
        functools.partial(_gla_kernel, q_scale=dk ** -0.5, n_heads=H_A, dk=dk, dv=dv),
        out_shape=jax.ShapeDtypeStruct((2, t, d), F32),
        grid=(n_batch, 2, ncc + nlc),
        in_specs=[pl.BlockSpec((CHUNK, qk), lambda b, dr, c: (rb(b, dr, c), 0)),
                  pl.BlockSpec((CHUNK, qk), lambda b, dr, c: (rb(b, dr, c), 1)),
                  pl.BlockSpec((CHUNK, d), lambda b, dr, c: (rb(b, dr, c), 1)),
                  pl.BlockSpec((CHUNK, LANES), lambda b, dr, c: (rb(b, dr, c), 0)),
                  pl.BlockSpec((None, LANES, qk), lambda b, dr, c: (dr, 0, 0)),
                  pl.BlockSpec((None, 1, qk), lambda b, dr, c: (dr, 0, 0)),
                  pl.BlockSpec((None, _GLA_A_ROWS, CHUNK), lambda b, dr, c: (dr, 0, 0)),
                  pl.BlockSpec((None, len(_GLA_LEVELS) + 1, CHUNK, CHUNK), lambda b, dr, c: (dr, 0, 0, 0))],
        out_specs=pl.BlockSpec((None, CHUNK, d), lambda b, dr, c: (dr, rb(b, dr, c), 0)),
        scratch_shapes=[pltpu.VMEM((H_A, dk, dv), F32)],
        compiler_params=_cparams("arbitrary", "arbitrary", "arbitrary"),
        name="gla_scan",
    )(z, z, z, g_low, wup_ext, b_g, a_c, mask_c)


def _mlstm_kernel(q_ref, k_ref, v_ref, gates_ref, o_ref, cn_ref, m_ref, *, q_scale, n_heads, dk, dv):
    dr = pl.program_id(1)
    c = pl.program_id(2)

    @pl.when(c == 0)
    def _():
        cn_ref[...] = jnp.zeros(cn_ref.shape, F32)
        m_ref[...] = jnp.full(m_ref.shape, M_INIT, F32)

    ii = lax.broadcasted_iota(jnp.int32, (CHUNK, CHUNK), 0)
    jj = lax.broadcasted_iota(jnp.int32, (CHUNK, CHUNK), 1)
    eye = ii == jj
    causal = (jj - ii) * (1 - 2 * dr) <= 0
    lane = lax.broadcasted_iota(jnp.int32, (CHUNK, LANES), 1)
    ones_col = jnp.where(lane == 0, 1.0, 0.0).astype(BF16)

    def to_col(r):
        return jnp.sum(jnp.where(eye, jnp.broadcast_to(r, (CHUNK, CHUNK)), 0.0), axis=1, keepdims=True)

    def to_row(cl):
        return jnp.sum(jnp.where(eye, jnp.broadcast_to(cl, (CHUNK, CHUNK)), 0.0), axis=0, keepdims=True)

    def gate_weights(h):
        ic = GATE_SOFTCAP * jnp.tanh(gates_ref[h] * (1.0 / GATE_SOFTCAP))
        fc = _log_sigmoid(GATE_SOFTCAP * jnp.tanh(gates_ref[n_heads + h] * (1.0 / GATE_SOFTCAP)))
        b_col = jnp.sum(jnp.where(causal, jnp.broadcast_to(fc, (CHUNK, CHUNK)), 0.0), axis=1, keepdims=True)
        b_row = to_row(b_col)
        m_old = m_ref[h][0:1, 0:1]
        d_log = jnp.where(causal, b_col - b_row + ic, NEG_INF)
        inter_log = b_col + m_old
        m_row = jnp.maximum(inter_log, jnp.max(d_log, axis=1, keepdims=True))
        w_intra = jnp.exp(d_log - m_row)
        w_inter = jnp.exp(inter_log - m_row)
        floor = jnp.exp(-m_row)
        b_last = jnp.sum(fc, axis=1, keepdims=True)
        w_log = b_last - b_row + ic
        m_new = jnp.maximum(b_last + m_old, jnp.max(w_log, axis=1, keepdims=True))
        keep = jnp.exp(b_last + m_old - m_new)
        w_col = to_col(jnp.exp(w_log - m_new))
        m_ref[h] = jnp.broadcast_to(m_new, m_ref.shape[1:])
        return w_intra, w_inter, floor, keep, w_col

    def output_and_state(h, w_intra, w_inter, floor, keep, w_col):
        ks = slice(h * dk, (h + 1) * dk)
        vs = slice(h * dv, (h + 1) * dv)
        q = (q_ref[:, ks].astype(F32) * q_scale).astype(BF16)
        k = k_ref[:, ks]
        v_ext = jnp.concatenate([v_ref[:, vs], ones_col], axis=1)
        s = _dot_nt(q, k) * w_intra
        cn_old = cn_ref[h]
        nd = _dot(s.astype(BF16), v_ext) + w_inter * _dot(q, cn_old.astype(BF16))
        den = nd[:, dv:dv + 1]
        o_ref[:, vs] = nd[:, :dv] / jnp.maximum(jnp.abs(den), floor)
        wk = k.astype(F32) * w_col
        wk_t = jnp.concatenate([wk, jnp.zeros_like(wk)], axis=0).T
        v_pad = jnp.concatenate([v_ext, jnp.zeros_like(v_ext)], axis=0)
        cn_ref[h] = keep * cn_old + _dot(wk_t.astype(BF16), v_pad)

    weights = {}
    for step in range(n_heads + 1):
        if step < n_heads:
            weights[step] = gate_weights(step)
        if step >= 1:
            output_and_state(step - 1, *weights.pop(step - 1))


def _mlstm_scan(z, gates, *, n_batch, seq, ctx_len, dk, dv):
    t = z.shape[0]
    qk, d = H_B * dk, H_B * dv
    assert 2 * qk == d
    ncc, nlc = ctx_len // CHUNK, seq // CHUNK
    rb = functools.partial(_chunk_row_block, n_ctx_chunks=ncc, n_lat_chunks=nlc, n_batch=n_batch)
    return pl.pallas_call(
        functools.partial(_mlstm_kernel, q_scale=dk ** -0.5, n_heads=H_B, dk=dk, dv=dv),
        out_shape=jax.ShapeDtypeStruct((2, t, d), F32),
        grid=(n_batch, 2, ncc + nlc),
        in_specs=[pl.BlockSpec((CHUNK, qk), lambda b, dr, c: (rb(b, dr, c), 0)),
                  pl.BlockSpec((CHUNK, qk), lambda b, dr, c: (rb(b, dr, c), 1)),
                  pl.BlockSpec((CHUNK, d), lambda b, dr, c: (rb(b, dr, c), 1)),
                  pl.BlockSpec((None, None, 2 * H_B, 1, CHUNK), lambda b, dr, c: (dr, rb(b, dr, c), 0, 0, 0))],
        out_specs=pl.BlockSpec((None, CHUNK, d), lambda b, dr, c: (dr, rb(b, dr, c), 0)),
        scratch_shapes=[pltpu.VMEM((H_B, dk, dv + LANES), F32), pltpu.VMEM((H_B, 8, LANES), F32)],
        compiler_params=_cparams("arbitrary", "arbitrary", "arbitrary"),
        name="mlstm_scan",
    )(z, z, z, gates)


def _readout_kernel(o_ref, g_ref, ng_ref, y_ref, *, n_heads, gate):
    o = o_ref[0] + o_ref[1]
    g = g_ref[...].astype(F32)
    act = g * _sigmoid(g) if gate == "silu" else _sigmoid(g)
    dh = o.shape[1] // n_heads
    for hh in range(n_heads):
        sl = slice(hh * dh, (hh + 1) * dh)
        oh = o[:, sl]
        r = lax.rsqrt(jnp.mean(oh * oh, axis=-1, keepdims=True) + RMS_EPS)
        y_ref[:, sl] = (oh * r * ng_ref[:, sl] * act[:, sl]).astype(y_ref.dtype)


def _readout(o2, z, norm_g, *, n_heads, gate, gate_col0, rows):
    d = o2.shape[2]
    tr = _pick_tile(rows, 128, 8)
    gblk = gate_col0 // d
    return pl.pallas_call(
        functools.partial(_readout_kernel, n_heads=n_heads, gate=gate),
        out_shape=jax.ShapeDtypeStruct((rows, d), BF16),
        grid=(rows // tr,),
        in_specs=[pl.BlockSpec((2, tr, d), lambda i: (0, i, 0)),
                  pl.BlockSpec((tr, d), lambda i: (i, gblk)),
                  pl.BlockSpec((1, d), lambda i: (0, 0))],
        out_specs=pl.BlockSpec((tr, d), lambda i: (i, 0)),
        compiler_params=_cparams("parallel"),
        name="readout",
    )(o2, z, norm_g.reshape(1, d))


def _flash_kernel(*refs, n_seg):
    q_ref = refs[0]
    kv = refs[1:1 + 2 * n_seg]
    o_ref = refs[1 + 2 * n_seg]
    vx_refs = refs[2 + 2 * n_seg:]
    q = q_ref[...]
    tq = q.shape[0]
    m = jnp.full((1, tq), NEG_INF, F32)
    acc = jnp.zeros((D_V_C + ONES_ROWS, tq), F32)

    @pl.when(pl.program_id(2) == 0)
    def _():
        for sg in range(n_seg):
            vx_refs[sg][:D_V_C, :] = kv[2 * sg + 1][...]
            vx_refs[sg][D_V_C:, :] = jnp.ones((ONES_ROWS, vx_refs[sg].shape[1]), BF16)

    blocks = []
    for sg in range(n_seg):
        k_ref, vt_ref = kv[2 * sg], vx_refs[sg]
        length = k_ref.shape[0]
        tk = min(KV_TILE, length)
        blocks += [(k_ref, vt_ref, t * tk, tk) for t in range(length // tk)]

    def scores(blk):
        k_ref, _, lo, tk = blk
        return _dot_nt(k_ref[lo:lo + tk, :], q)

    st_next = scores(blocks[0])
    pending = None
    for t, (_, vt_ref, lo, tk) in enumerate(blocks):
        st = st_next
        if t + 1 < len(blocks):
            st_next = scores(blocks[t + 1])
        m_new = jnp.maximum(m, jnp.max(st, axis=0, keepdims=True))
        alpha = jnp.exp2(m - m_new)
        p = jnp.exp2(st - m_new).astype(BF16)
        m = m_new
        if pending is not None:
            a_prev, vt_prev, p_prev = pending
            acc = a_prev * acc + _dot(vt_prev, p_prev)
        pending = (alpha, vt_ref[:, lo:lo + tk], p)
    a_prev, vt_prev, p_prev = pending
    acc = a_prev * acc + _dot(vt_prev, p_prev)
    o_ref[...] = (acc[:D_V_C] / acc[D_V_C:D_V_C + 1]).T.astype(o_ref.dtype)


def _flash(qcat, kcat, vt, *, n_batch, q_rows0, q_len, segs):
    tq = _pick_tile(q_len, 512, 8)
    nq = q_len // tq
    in_specs = [pl.BlockSpec((tq, MXU_DIM), lambda b, h, i: ((q_rows0 + b * q_len) // tq + i, h))]
    args = [qcat]
    for row0, length in segs:
        in_specs.append(pl.BlockSpec((length, MXU_DIM), lambda b, h, i, row0=row0, length=length: (row0 // length + b, h)))
        in_specs.append(pl.BlockSpec((D_V_C, length), lambda b, h, i, row0=row0, length=length: (h, row0 // length + b)))
        args += [kcat, vt]
    return pl.pallas_call(
        functools.partial(_flash_kernel, n_seg=len(segs)),
        out_shape=jax.ShapeDtypeStruct((n_batch * q_len, H_C * D_V_C), BF16),
        grid=(n_batch, H_C, nq),
        in_specs=in_specs,
        out_specs=pl.BlockSpec((tq, D_V_C), lambda b, h, i: (b * nq + i, h)),
        scratch_shapes=[pltpu.VMEM((D_V_C + ONES_ROWS, length), BF16) for _, length in segs],
        compiler_params=_cparams("parallel", "parallel", "arbitrary"),
        name="flash",
    )(*args)


def _mla_prep_kernel(z_ref, gq_ref, gkv_ref, cos_ref, sin_ref, cq_ref, akv_ref, *, q_rank, kv_rank):
    cq = z_ref[:, :q_rank]
    cq_ref[...] = (cq * lax.rsqrt(jnp.mean(cq * cq, axis=-1, keepdims=True) + RMS_EPS) * gq_ref[...]).astype(BF16)
    ckv = z_ref[:, q_rank:q_rank + kv_rank]
    akv_ref[:, :kv_rank] = (ckv * lax.rsqrt(jnp.mean(ckv * ckv, axis=-1, keepdims=True) + RMS_EPS)
                            * gkv_ref[...]).astype(BF16)
    x = z_ref[:, q_rank + kv_rank:q_rank + kv_rank + LANES]
    lane = lax.broadcasted_iota(jnp.int32, x.shape, 1)
    swapped = jnp.where(lane < D_ROPE // 2, pltpu.roll(x, LANES - D_ROPE // 2, 1), pltpu.roll(x, D_ROPE // 2, 1))
    akv_ref[:, kv_rank:] = (x * cos_ref[...] + swapped * sin_ref[...]).astype(BF16)


def _mla_prep(zd, gq, gkv, cos, sin, *, q_rank, kv_rank):
    t = zd.shape[0]
    tr = _pick_tile(t, 256, 8)
    return pl.pallas_call(
        functools.partial(_mla_prep_kernel, q_rank=q_rank, kv_rank=kv_rank),
        out_shape=(jax.ShapeDtypeStruct((t, q_rank), BF16), jax.ShapeDtypeStruct((t, kv_rank + LANES), BF16)),
        grid=(t // tr,),
        in_specs=[pl.BlockSpec((tr, zd.shape[1]), lambda i: (i, 0)),
                  pl.BlockSpec((1, q_rank), lambda i: (0, 0)),
                  pl.BlockSpec((1, kv_rank), lambda i: (0, 0)),
                  pl.BlockSpec((tr, LANES), lambda i: (i, 0)),
                  pl.BlockSpec((tr, LANES), lambda i: (i, 0))],
        out_specs=(pl.BlockSpec((tr, q_rank), lambda i: (i, 0)),
                   pl.BlockSpec((tr, kv_rank + LANES), lambda i: (i, 0))),
        compiler_params=_cparams("parallel"),
        name="mla_prep",
    )(zd, gq.reshape(1, q_rank), gkv.reshape(1, kv_rank), cos, sin)


def _rope_tables(n_batch, seq, ctx_len):
    n_rows = seq // GRID_W
    rows = np.repeat(np.arange(n_rows, dtype=np.float32), GRID_W)
    cols = np.tile(np.arange(GRID_W, dtype=np.float32), n_rows)
    half = D_ROPE // 2
    inv_freq = ROPE_THETA ** (-jnp.arange(0, half, 2, dtype=F32) / half)
    ang = jnp.concatenate([jnp.asarray(rows)[:, None] * inv_freq, jnp.asarray(cols)[:, None] * inv_freq], axis=-1)
    cos, sin = jnp.cos(ang), jnp.sin(ang)
    pad1 = jnp.ones((seq, LANES - D_ROPE), F32)
    pad0 = jnp.zeros((seq, LANES - D_ROPE), F32)
    cos_l = jnp.concatenate([cos, cos, pad1], axis=1)
    sin_l = jnp.concatenate([-sin, sin, pad0], axis=1)
    cos_t = jnp.concatenate([jnp.tile(cos_l, (n_batch, 1)), jnp.ones((n_batch * ctx_len, LANES), F32)], axis=0)
    sin_t = jnp.concatenate([jnp.tile(sin_l, (n_batch, 1)), jnp.zeros((n_batch * ctx_len, LANES), F32)], axis=0)
    return cos_t, sin_t


def _gla_mixer(h, w_in_b, j, w_gdown, w_gup, b_g, norm_g, dims):
    n_batch, seq, ctx_len, d = dims
    dk, dv = d // (2 * H_A), d // H_A
    qk = H_A * dk
    z = _mm(h, w_in_b, w_lead=(j,), out_dtype=BF16)
    w_low = jnp.concatenate([w_gdown[0], w_gdown[1],
                             jnp.zeros((d, LANES - 2 * GATE_RANK_A), F32)], axis=1).astype(BF16)
    g_low = _mm(h, w_low, out_dtype=BF16)
    wup = jnp.zeros((2, LANES, qk), F32)
    wup = wup.at[0, :GATE_RANK_A].set(w_gup[0]).at[1, GATE_RANK_A:2 * GATE_RANK_A].set(w_gup[1]).astype(BF16)
    o2 = _gla_scan(z, g_low, wup, b_g.reshape(2, 1, qk), n_batch=n_batch, seq=seq, ctx_len=ctx_len, dk=dk, dv=dv)
    return _readout(o2, z, norm_g, n_heads=H_A, gate="silu", gate_col0=2 * qk + d, rows=z.shape[0])


def _mlstm_mixer(h, w_in_b, j, w_if, b_if, norm_g, dims):
    n_batch, seq, ctx_len, d = dims
    dk, dv = d // (2 * H_B), d // H_B
    qk = H_B * dk
    t = h.shape[0]
    z = _mm(h, w_in_b, w_lead=(j,), out_dtype=BF16)
    w_g = jnp.concatenate([w_if[0], w_if[1], jnp.zeros((d, LANES - 4 * H_B), F32)], axis=1).astype(BF16)
    pre = _mm(h, w_g, out_dtype=F32)[:, :4 * H_B] + jnp.concatenate([b_if[0], b_if[1]])[None, :]
    gates = pre.reshape(t // CHUNK, CHUNK, 2, 2 * H_B).transpose(2, 0, 3, 1)[:, :, :, None, :]
    o2 = _mlstm_scan(z, gates, n_batch=n_batch, seq=seq, ctx_len=ctx_len, dk=dk, dv=dv)
    return _readout(o2, z, norm_g, n_heads=H_B, gate="sigmoid", gate_col0=2 * qk + d, rows=t)


def _mla_mixer(h, w_down, q_norm_g, w_uq, kv_norm_g, w_ukv, dims, need_ctx_out):
    n_batch, seq, ctx_len, d = dims
    q_rank, kv_rank = d // 4, d // 8
    n_lat = n_batch * seq
    scale = (D_NOPE + D_ROPE) ** -0.5 * LOG2_E
    cos_t, sin_t = _rope_tables(n_batch, seq, ctx_len)

    n_down = q_rank + kv_rank + LANES
    w_down_ext = jnp.concatenate([w_down, jnp.zeros((d, n_down - w_down.shape[1]), F32)], axis=1).astype(BF16)
    zd = _mm(h, w_down_ext, out_dtype=F32)
    cqn, akv = _mla_prep(zd, q_norm_g, kv_norm_g, cos_t, sin_t, q_rank=q_rank, kv_rank=kv_rank)

    wq = w_uq.reshape(q_rank, H_C, D_NOPE + D_ROPE)
    wq = jnp.concatenate([wq, jnp.zeros((q_rank, H_C, MXU_DIM - D_NOPE - D_ROPE), F32)], axis=2)
    qcat = _mm(cqn, wq.reshape(q_rank, H_C * MXU_DIM).astype(BF16), out_dtype=BF16,
               epilogue="rope", cos=cos_t, sin=sin_t, scale=scale)

    wkv = w_ukv.reshape(kv_rank, H_C, D_NOPE + D_V_C)
    wk_top = jnp.concatenate([wkv[:, :, :D_NOPE], jnp.zeros((kv_rank, H_C, MXU_DIM - D_NOPE), F32)], axis=2)
    ident = jnp.concatenate([jnp.zeros((LANES, D_NOPE), F32), jnp.eye(LANES, dtype=F32)], axis=1)
    wk_ext = jnp.concatenate([wk_top, jnp.broadcast_to(ident[:, None, :], (LANES, H_C, MXU_DIM))], axis=0)
    kcat = _mm(akv, wk_ext.reshape(kv_rank + LANES, H_C * MXU_DIM).astype(BF16), out_dtype=BF16)
    wv_t = jnp.concatenate([wkv[:, :, D_NOPE:].reshape(kv_rank, H_C * D_V_C).T,
                            jnp.zeros((H_C * D_V_C, LANES), F32)], axis=1)
    vt = _mm(wv_t.astype(BF16), akv.T, out_dtype=BF16)

    o_lat = _flash(qcat, kcat, vt, n_batch=n_batch, q_rows0=0, q_len=seq, segs=[(0, seq), (n_lat, ctx_len)])
    if not need_ctx_out:
        return o_lat
    o_ctx = _flash(qcat, kcat, vt, n_batch=n_batch, q_rows0=n_lat, q_len=ctx_len, segs=[(n_lat, ctx_len)])
    return jnp.concatenate([o_lat, o_ctx], axis=0)


def _modulations(cond, ada_down, ada_up, ada_bias, i):
    d = cond.shape[1]
    low = _mm(cond, ada_down, w_lead=(i,), out_dtype=BF16, prologue="silu")
    m = _mm(low, ada_up, w_lead=(i,), out_dtype=F32) + ada_bias[i][None, :]
    return m.reshape(cond.shape[0], N_MOD, d)


def kernel(x, c, ctx, c_ctx, ada_down, ada_up, ada_bias, ln_g, ln_b, ffn_w_in, ffn_w_out, gla_w_in, gla_w_gdown, gla_w_gup, gla_b_g, gla_norm_g, gla_w_out, mlstm_w_in, mlstm_w_if, mlstm_b_if, mlstm_norm_g, mlstm_w_out, mla_w_down, mla_q_norm_g, mla_w_uq, mla_kv_norm_g, mla_w_ukv, mla_w_out):
    n_batch, seq, d = x.shape
    ctx_len = ctx.shape[1]
    depth = ada_down.shape[0]
    n_lat = n_batch * seq
    n_ctx = n_batch * ctx_len
    t_all = n_lat + n_ctx
    dims = (n_batch, seq, ctx_len, d)
    alpha = (2.0 * depth) ** 0.25
    n_seg = n_batch + 1
    assert seq % ELEMENTWISE_TILE_ROWS == 0 and n_ctx % math.gcd(n_ctx, ELEMENTWISE_TILE_ROWS) == 0

    def seg_of_row(row):
        return jnp.minimum(row // seq, n_batch)

    xs = jnp.concatenate([x.reshape(n_lat, d), ctx.reshape(n_ctx, d)], axis=0)
    cond = jnp.concatenate([c, c_ctx[None, :], jnp.zeros((8 - n_seg, d), F32)], axis=0)


    def mod_vecs(m, s, weight):
        shift = m[:n_seg, 3 * s][:, None, :]
        scale1p = 1.0 + m[:n_seg, 3 * s + 1][:, None, :]
        gate_w = weight * m[:n_seg, 3 * s + 2][:, None, :]
        return shift, scale1p, gate_w

    mods = [_modulations(cond, ada_down, ada_up, ada_bias, i) for i in range(depth)]
    shift0, scale0, _ = mod_vecs(mods[0], 0, MACARON_WEIGHT)
    hcur = _modulate(xs, scale0, shift0, seg_of_row=seg_of_row)

    def close(y_in, w, w_lead, xs, gate_w, g, b, scale1p, shift, rows):
        y = _mm(y_in, w, w_lead=w_lead, rows=rows, out_dtype=F32)
        return _res_ln(y, xs, gate_w, g, b, scale1p, shift, rows=rows, seg_of_row=seg_of_row, alpha=alpha)

    for i in range(depth):
        last = i == depth - 1
        m = mods[i]
        _, _, gw0 = mod_vecs(m, 0, MACARON_WEIGHT)
        sh1, sc1, gw1 = mod_vecs(m, 1, 1.0)
        sh2, sc2, gw2 = mod_vecs(m, 2, MACARON_WEIGHT)

        act = _ffn1(hcur, ffn_w_in, t_all, w_lead=(i, 0))
        xs, hcur = close(act, ffn_w_out, (i, 0), xs, gw0, ln_g[i, 0], ln_b[i, 0], sc1, sh1, t_all)

        kind, j = i % 3, i // 3
        rows1 = n_lat if last else t_all
        if kind == 0:
            y_in = _gla_mixer(hcur, gla_w_in, j, gla_w_gdown[j], gla_w_gup[j], gla_b_g[j], gla_norm_g[j], dims)
            w_o = gla_w_out
        elif kind == 1:
            y_in = _mlstm_mixer(hcur, mlstm_w_in, j, mlstm_w_if[j], mlstm_b_if[j], mlstm_norm_g[j], dims)
            w_o = mlstm_w_out
        else:
            y_in = _mla_mixer(hcur, mla_w_down[j], mla_q_norm_g[j], mla_w_uq[j], mla_kv_norm_g[j], mla_w_ukv[j],
                              dims, not last)
            w_o = mla_w_out
        xs, hcur = close(y_in, w_o, (j,), xs, gw1, ln_g[i, 1], ln_b[i, 1], sc2, sh2, rows1)

        if last:
            sh_n, sc_n = sh2, sc2
        else:
            sh_n, sc_n, _ = mod_vecs(mods[i + 1], 0, MACARON_WEIGHT)
        act = _ffn1(hcur, ffn_w_in, rows1, w_lead=(i, 1))
        xs, hcur = close(act, ffn_w_out, (i, 1), xs, gw2, ln_g[i, 2], ln_b[i, 2], sc_n, sh_n, rows1)

    return xs.reshape(n_batch, seq, d)
```

```python
import functools
import math

import numpy as np
import jax
import jax.numpy as jnp
from jax import lax
from jax.experimental import pallas as pl
from jax.experimental.pallas import tpu as pltpu

F32 = jnp.float32
BF16 = jnp.bfloat16

N_MOD = 9
MACARON_WEIGHT = 0.5
LN_EPS = 1e-5
RMS_EPS = 1e-6
CHUNK = 64
H_A = 8
GATE_RANK_A = 16
GATE_NORMALIZER_A = 16.0
H_B = 8
GATE_SOFTCAP = 15.0
M_INIT = -1e30
H_C = 32
D_NOPE = 128
D_ROPE = 64
D_V_C = 128
ROPE_THETA = 10000.0
GRID_W = 64

LANES = 128
MXU_DIM = 256
VMEM_LIMIT_BYTES = 60 * 1024 * 1024
ROW_TILE_TARGET = 512
COL_TILE_TARGET = 2048
MM_VMEM_BUDGET = 48 * 1024 * 1024
ELEMENTWISE_TILE_ROWS = 256
LN_ROWS = 32
PROJ_K_TILE = 1280
KV_TILE = 512
ONES_ROWS = 16

NEG_INF = float("-inf")
LOG2_E = math.log2(math.e)


def _cparams(*sem):
    return pltpu.CompilerParams(dimension_semantics=sem, vmem_limit_bytes=VMEM_LIMIT_BYTES)


def _pick_tile(n, target, mult=LANES):
    if n <= target:
        return n
    t = (target // mult) * mult
    while t > mult and n % t:
        t -= mult
    assert n % t == 0, (n, target, mult)
    return t


def _sigmoid(x):
    return 1.0 / (1.0 + jnp.exp(-x))


def _log_sigmoid(x):
    return jnp.minimum(x, 0.0) - jnp.log1p(jnp.exp(-jnp.abs(x)))


def _dot(a, b):
    return jnp.dot(a, b, preferred_element_type=F32)


def _dot_nt(a, b):
    return lax.dot_general(a, b, (((1,), (1,)), ((), ())), preferred_element_type=F32)


def _split2(x):
    hi = x.astype(BF16)
    lo = (x - hi.astype(F32)).astype(BF16)
    return hi, lo


def _wspec(lead, block, index_map):
    return pl.BlockSpec((None,) * len(lead) + block, lambda *g: tuple(lead) + index_map(*g))


def _resident_weights(w_ref, wb_ref):
    @pl.when(pl.program_id(1) == 0)
    def _():
        def body(r, carry):
            rows = pl.ds(pl.multiple_of(r * MXU_DIM, MXU_DIM), MXU_DIM)
            wb_ref[rows, :] = w_ref[rows, :].astype(wb_ref.dtype)
            return carry
        lax.fori_loop(0, w_ref.shape[0] // MXU_DIM, body, 0)
    return wb_ref


def _mm_kernel(*refs, prologue, epilogue, scale, cast_w):
    a_ref, w_ref = refs[0], refs[1]
    if cast_w:
        w_ref = _resident_weights(w_ref, refs[-1])
        refs = refs[:-1]
    o_ref = refs[-1]
    extra = refs[2:-1]
    a = a_ref[...]
    if prologue == "silu":
        af = a.astype(F32)
        a = (af * _sigmoid(af)).astype(BF16)
    acc = _dot(a, w_ref[...])
    if epilogue == "rope":
        cos = extra[0][...]
        sin = extra[1][...]
        lane = lax.broadcasted_iota(jnp.int32, cos.shape, 1)
        first_half = lane < (D_ROPE // 2)
        for g in range(acc.shape[1] // MXU_DIM):
            lo = g * MXU_DIM
            o_ref[:, lo:lo + LANES] = (acc[:, lo:lo + LANES] * scale).astype(o_ref.dtype)
            x = acc[:, lo + LANES:lo + 2 * LANES]
            swapped = jnp.where(first_half, pltpu.roll(x, LANES - D_ROPE // 2, 1), pltpu.roll(x, D_ROPE // 2, 1))
            o_ref[:, lo + LANES:lo + 2 * LANES] = ((x * cos + swapped * sin) * scale).astype(o_ref.dtype)
    else:
        o_ref[...] = acc.astype(o_ref.dtype)


def _mm_tiles(m, kk, n, a_bytes, w_bytes, out_bytes, cast_w, tn_mult):
    tm = _pick_tile(m, ROW_TILE_TARGET, 8)
    tn_target = COL_TILE_TARGET
    while True:
        tn = _pick_tile(n, tn_target, tn_mult)
        need = 2 * tm * kk * a_bytes + 2 * kk * tn * w_bytes + 2 * tm * tn * out_bytes + tm * tn * 4
        need += kk * tn * 2 if cast_w else 0
        if need <= MM_VMEM_BUDGET or tn <= tn_mult:
            return tm, tn
        tn_target = tn - tn_mult


def _mm(a, w, *, out_dtype, w_lead=(), rows=None, prologue=None, epilogue=None, cos=None, sin=None, scale=1.0):
    m = a.shape[0] if rows is None else rows
    kk, n = w.shape[-2:]
    assert a.shape[1] == kk or kk % LANES == 0
    cast_w = w.dtype != BF16
    tm, tn = _mm_tiles(m, kk, n, a.dtype.itemsize, w.dtype.itemsize, jnp.dtype(out_dtype).itemsize, cast_w,
                       MXU_DIM if epilogue == "rope" else LANES)
    in_specs = [pl.BlockSpec((tm, kk), lambda j, i: (i, 0)),
                _wspec(w_lead, (kk, tn), lambda j, i: (0, j))]
    args = [a, w]
    if epilogue == "rope":
        in_specs += [pl.BlockSpec((tm, LANES), lambda j, i: (i, 0))] * 2
        args += [cos, sin]
    return pl.pallas_call(
        functools.partial(_mm_kernel, prologue=prologue, epilogue=epilogue, scale=scale, cast_w=cast_w),
        out_shape=jax.ShapeDtypeStruct((m, n), out_dtype),
        grid=(n // tn, m // tm),
        in_specs=in_specs,
        out_specs=pl.BlockSpec((tm, tn), lambda j, i: (i, j)),
        scratch_shapes=[pltpu.VMEM((kk, tn), BF16)] if cast_w else [],
        compiler_params=_cparams("parallel", "arbitrary"),
        name="mm_" + (prologue or "p") + "_" + (epilogue or "e"),
    )(*args)


def _ffn1_kernel(h_ref, wa_ref, wu_ref, o_ref, *scratch):
    if scratch:
        wa_ref = _resident_weights(wa_ref, scratch[0])
        wu_ref = _resident_weights(wu_ref, scratch[1])
    h = h_ref[...]
    a = _dot(h, wa_ref[...])
    u = _dot(h, wu_ref[...])
    o_ref[...] = (a * _sigmoid(a) * u).astype(o_ref.dtype)


def _ffn1(h, w_in, rows, w_lead=()):
    d, f2 = w_in.shape[-2:]
    f = f2 // 2
    cast_w = w_in.dtype != BF16
    tm = _pick_tile(rows, ROW_TILE_TARGET, 8)
    tn = _pick_tile(f, 512)
    nj = f // tn
    return pl.pallas_call(
        _ffn1_kernel,
        out_shape=jax.ShapeDtypeStruct((rows, f), BF16),
        grid=(nj, rows // tm),
        in_specs=[pl.BlockSpec((tm, d), lambda j, i: (i, 0)),
                  _wspec(w_lead, (d, tn), lambda j, i: (0, j)),
                  _wspec(w_lead, (d, tn), lambda j, i: (0, j + nj))],
        out_specs=pl.BlockSpec((tm, tn), lambda j, i: (i, j)),
        scratch_shapes=[pltpu.VMEM((d, tn), BF16)] * 2 if cast_w else [],
        compiler_params=_cparams("parallel", "arbitrary"),
        name="ffn1",
    )(h, w_in, w_in)


def _proj_ln_kernel(act_ref, w_ref, x_ref, gw_ref, lng_ref, lnb_ref, sc_ref, sh_ref, xo_ref, ho_ref,
                    acc_even_ref, acc_odd_ref, *, alpha, n_tiles):
    i = pl.program_id(0)
    k = pl.program_id(1)
    slab = xo_ref.shape[0]
    accs = (acc_even_ref, acc_odd_ref)

    def accumulate(cur_ref):
        cur_ref[...] = _dot(act_ref[...], w_ref[...]) + cur_ref[...]

    def normalise(prev_ref):
        for r in range(slab // LN_ROWS):
            lo = r * LN_ROWS
            rows = pl.ds(pl.multiple_of(k * slab + lo, LN_ROWS), LN_ROWS)
            v = alpha * x_ref[lo:lo + LN_ROWS, :] + gw_ref[...] * prev_ref[rows, :]
            mu = jnp.mean(v, axis=-1, keepdims=True)
            vc = v - mu
            var = jnp.mean(vc * vc, axis=-1, keepdims=True)
            xn = vc * lax.rsqrt(var + LN_EPS) * lng_ref[...] + lnb_ref[...]
            xo_ref[lo:lo + LN_ROWS, :] = xn
            ho_ref[lo:lo + LN_ROWS, :] = (xn * sc_ref[...] + sh_ref[...]).astype(ho_ref.dtype)
            prev_ref[rows, :] = jnp.zeros((LN_ROWS, prev_ref.shape[1]), F32)

    @pl.when(jnp.logical_and(i == 0, k == 0))
    def _():
        acc_even_ref[...] = jnp.zeros(acc_even_ref.shape, F32)
        acc_odd_ref[...] = jnp.zeros(acc_odd_ref.shape, F32)

    @pl.when(i == 0)
    def _():
        accumulate(acc_even_ref)

    for parity in range(2):
        @pl.when(jnp.logical_and(jnp.logical_and(i > 0, i < n_tiles), lax.rem(i, 2) == parity))
        def _(parity=parity):
            accumulate(accs[parity])
            normalise(accs[1 - parity])

    @pl.when(i == n_tiles)
    def _():
        normalise(accs[(n_tiles - 1) % 2])


def _proj_ln(act, w, x, gate_w, ln_g, ln_b, scale1p, shift, *, rows, seg_of_row, seg_rows, alpha, w_lead=()):
    kk, d = w.shape[-2:]
    tm = math.gcd(math.gcd(rows, ROW_TILE_TARGET), seg_rows)
    n_tiles = rows // tm
    tk = kk
    for cand in range((min(PROJ_K_TILE, kk) // MXU_DIM) * MXU_DIM, 0, -MXU_DIM):
        if kk % cand == 0 and tm % ((kk // cand) * LN_ROWS) == 0:
            tk = cand
            break
    n_k = kk // tk
    slab_rows = tm // n_k
    assert tm % LN_ROWS == 0

    def slab_index(i, k):
        return jnp.where(i == 0, 0, (i - 1) * n_k + k)

    vec = pl.BlockSpec((None, 1, d), lambda i, k: (seg_of_row(slab_index(i, k) * slab_rows), 0, 0))
    cvec = pl.BlockSpec((1, d), lambda i, k: (0, 0))
    slab = pl.BlockSpec((slab_rows, d), lambda i, k: (slab_index(i, k), 0))
    return pl.pallas_call(
        functools.partial(_proj_ln_kernel, alpha=alpha, n_tiles=n_tiles),
        out_shape=(jax.ShapeDtypeStruct((rows, d), F32), jax.ShapeDtypeStruct((rows, d), BF16)),
        grid=(n_tiles + 1, n_k),
        in_specs=[pl.BlockSpec((tm, tk), lambda i, k: (jnp.minimum(i, n_tiles - 1), k)),
                  _wspec(w_lead, (tk, d), lambda i, k: (k, 0)),
                  slab, vec, cvec, cvec, vec, vec],
        out_specs=(slab, slab),
        scratch_shapes=[pltpu.VMEM((tm, d), F32), pltpu.VMEM((tm, d), F32)],
        compiler_params=_cparams("arbitrary", "arbitrary"),
        name="proj_ln",
    )(act, w, x, gate_w, ln_g.reshape(1, d), ln_b.reshape(1, d), scale1p, shift)


def _modulate_kernel(x_ref, sc_ref, sh_ref, o_ref):
    o_ref[...] = (x_ref[...] * sc_ref[...] + sh_ref[...]).astype(o_ref.dtype)


def _modulate(x, scale1p, shift, *, seg_of_row):
    t, d = x.shape
    tr = math.gcd(t, ELEMENTWISE_TILE_ROWS)
    vec = pl.BlockSpec((None, 1, d), lambda i: (seg_of_row(i * tr), 0, 0))
    return pl.pallas_call(
        _modulate_kernel,
        out_shape=jax.ShapeDtypeStruct((t, d), BF16),
        grid=(t // tr,),
        in_specs=[pl.BlockSpec((tr, d), lambda i: (i, 0)), vec, vec],
        out_specs=pl.BlockSpec((tr, d), lambda i: (i, 0)),
        compiler_params=_cparams("parallel"),
        name="modulate",
    )(x, scale1p, shift)


def _chunk_row_block(b, dr, c, *, n_ctx_chunks, n_lat_chunks, n_batch):
    jc = jnp.where(dr == 1, n_ctx_chunks - 1 - c, c)
    jl = jnp.where(dr == 1, n_lat_chunks - 1 - (c - n_ctx_chunks), c - n_ctx_chunks)
    ctx_blk = n_batch * n_lat_chunks + b * n_ctx_chunks + jc
    lat_blk = b * n_lat_chunks + jl
    return jnp.where(c < n_ctx_chunks, ctx_blk, lat_blk)


_GLA_LEVELS = (32, 16, 8, 4, 2, 1)
_GLA_ROWS_B = len(_GLA_LEVELS) * CHUNK
_GLA_ROWS_BL = _GLA_ROWS_B + CHUNK
_GLA_ROWS_ONE = _GLA_ROWS_BL + CHUNK
_GLA_A_ROWS = _GLA_ROWS_ONE + CHUNK


def _gla_constants():
    c = CHUNK
    a = np.zeros((_GLA_A_ROWS, c), np.float32)
    masks = np.zeros((len(_GLA_LEVELS) + 1, c, c), np.float32)
    t = np.arange(c)
    for l, s in enumerate(_GLA_LEVELS):
        for i in range(c):
            r = (i // (2 * s)) * 2 * s + s - 1
            if i % (2 * s) >= s:
                a[l * c + i] = (t > r) & (t <= i)
                lo = (i // (2 * s)) * 2 * s
                masks[l, i, lo:lo + s] = 1.0
            else:
                a[l * c + i] = (t > i) & (t <= r)
    for i in range(c):
        a[_GLA_ROWS_B + i] = t <= i
        a[_GLA_ROWS_BL + i] = t > i
        a[_GLA_ROWS_ONE + i] = 1.0
    masks[len(_GLA_LEVELS)] = np.eye(c)
    a_rev = a.reshape(-1, c, c)[:, ::-1, ::-1].reshape(-1, c)
    masks_rev = masks[:, ::-1, ::-1]
    return (np.stack([a, a_rev]).astype(np.float32), np.stack([masks, masks_rev]).astype(np.float32))


def _gla_kernel(q_ref, k_ref, v_ref, g_ref, wup_ref, bg_ref, a_ref, mask_ref, o_ref, s_ref, *,
                q_scale, n_heads, dk, dv):
    dr = pl.program_id(1)
    c = pl.program_id(2)

    @pl.when(c == 0)
    def _():
        s_ref[...] = jnp.zeros(s_ref.shape, F32)

    pre = _dot(g_ref[...], wup_ref[...]) + bg_ref[...]
    la_all = _log_sigmoid(pre) * (1.0 / GATE_NORMALIZER_A)
    amat = a_ref[...]
    row = lax.broadcasted_iota(jnp.int32, (CHUNK, 1), 0)
    row = jnp.where(dr == 1, CHUNK - 1 - row, row)
    reps = dv // LANES

    def exponents(h):
        hi, lo = _split2(la_all[:, h * dk:(h + 1) * dk])
        return jnp.exp(_dot(amat, hi) + _dot(amat, lo))

    def intra(h, e):
        ks = slice(h * dk, (h + 1) * dk)
        qf = q_ref[:, ks].astype(F32) * q_scale
        kf = k_ref[:, ks].astype(F32)
        p = mask_ref[len(_GLA_LEVELS)] * _dot_nt(qf.astype(BF16), kf.astype(BF16))
        for l, s in enumerate(_GLA_LEVELS):
            upper = (row & s) != 0
            m = (jnp.where(upper, qf, kf) * e[l * CHUNK:(l + 1) * CHUNK]).astype(BF16)
            p = p + mask_ref[l] * _dot_nt(m, m)
        q_hat = (qf * e[_GLA_ROWS_B:_GLA_ROWS_B + CHUNK]).astype(BF16)
        k_hat = kf * e[_GLA_ROWS_BL:_GLA_ROWS_BL + CHUNK]
        ke_t = jnp.concatenate([k_hat, e[_GLA_ROWS_ONE:_GLA_ROWS_ONE + CHUNK]], axis=0).T
        return p.astype(BF16), q_hat, ke_t

    def output_and_state(h, p, q_hat, ke_t):
        vs = slice(h * dv, (h + 1) * dv)
        v = v_ref[:, vs]
        s_old = s_ref[h]
        o_ref[:, vs] = _dot(p, v) + _dot(q_hat, s_old.astype(BF16))
        decay = jnp.concatenate([ke_t[:, CHUNK:], ke_t[:, CHUNK:]], axis=1)
        v_pad = jnp.concatenate([v, jnp.zeros_like(v)], axis=0)
        s_ref[h] = s_old * jnp.concatenate([decay] * reps, axis=1) + _dot(ke_t.astype(BF16), v_pad)

    e_of, intra_of = {}, {}
    for step in range(n_heads + 2):
        if step < n_heads:
            e_of[step] = exponents(step)
        if 0 <= step - 1 < n_heads:
            intra_of[step - 1] = intra(step - 1, e_of.pop(step - 1))
        if 0 <= step - 2 < n_heads:
            output_and_state(step - 2, *intra_of.pop(step - 2))


def _gla_scan(z, g_low, wup_ext, b_g, *, n_batch, seq, ctx_len, dk, dv):
    t = z.shape[0]
    qk, d = H_A * dk, H_A * dv
    assert 2 * qk == d
    ncc, nlc = ctx_len // CHUNK, seq // CHUNK
    a_np, mask_np = _gla_constants()
    a_c = jnp.asarray(a_np, BF16)
    mask_c = jnp.asarray(mask_np, F32)
    rb = functools.partial(_chunk_row_block, n_ctx_chunks=ncc, n_lat_chunks=nlc, n_batch=n_batch)
    return pl.pallas_call(
        functools.partial(_gla_kernel, q_scale=dk ** -0.5, n_heads=H_A, dk=dk, dv=dv),
        out_shape=jax.ShapeDtypeStruct((2, t, d), F32),
        grid=(n_batch, 2, ncc + nlc),
        in_specs=[pl.BlockSpec((CHUNK, qk), lambda b, dr, c: (rb(b, dr, c), 0)),
                  pl.BlockSpec((CHUNK, qk), lambda b, dr, c: (rb(b, dr, c), 1)),
                  pl.BlockSpec((CHUNK, d), lambda b, dr, c: (rb(b, dr, c), 1)),
                  pl.BlockSpec((CHUNK, LANES), lambda b, dr, c: (rb(b, dr, c), 0)),
                  pl.BlockSpec((None, LANES, qk), lambda b, dr, c: (dr, 0, 0)),
                  pl.BlockSpec((None, 1, qk), lambda b, dr, c: (dr, 0, 0)),
                  pl.BlockSpec((None, _GLA_A_ROWS, CHUNK), lambda b, dr, c: (dr, 0, 0)),
                  pl.BlockSpec((None, len(_GLA_LEVELS) + 1, CHUNK, CHUNK), lambda b, dr, c: (dr, 0, 0, 0))],
        out_specs=pl.BlockSpec((None, CHUNK, d), lambda b, dr, c: (dr, rb(b, dr, c), 0)),
        scratch_shapes=[pltpu.VMEM((H_A, dk, dv), F32)],
        compiler_params=_cparams("arbitrary", "arbitrary", "arbitrary"),
        name="gla_scan",
    )(z, z, z, g_low, wup_ext, b_g, a_c, mask_c)


def _mlstm_kernel(q_ref, k_ref, v_ref, gates_ref, o_ref, cn_ref, m_ref, *, q_scale, n_heads, dk, dv):
    dr = pl.program_id(1)
    c = pl.program_id(2)

    @pl.when(c == 0)
    def _():
        cn_ref[...] = jnp.zeros(cn_ref.shape, F32)
        m_ref[...] = jnp.full(m_ref.shape, M_INIT, F32)

    ii = lax.broadcasted_iota(jnp.int32, (CHUNK, CHUNK), 0)
    jj = lax.broadcasted_iota(jnp.int32, (CHUNK, CHUNK), 1)
    eye = ii == jj
    causal = (jj - ii) * (1 - 2 * dr) <= 0
    lane = lax.broadcasted_iota(jnp.int32, (CHUNK, LANES), 1)
    ones_col = jnp.where(lane == 0, 1.0, 0.0).astype(BF16)

    def to_col(r):
        return jnp.sum(jnp.where(eye, jnp.broadcast_to(r, (CHUNK, CHUNK)), 0.0), axis=1, keepdims=True)

    def to_row(cl):
        return jnp.sum(jnp.where(eye, jnp.broadcast_to(cl, (CHUNK, CHUNK)), 0.0), axis=0, keepdims=True)

    def gate_weights(h):
        ic = GATE_SOFTCAP * jnp.tanh(gates_ref[h] * (1.0 / GATE_SOFTCAP))
        fc = _log_sigmoid(GATE_SOFTCAP * jnp.tanh(gates_ref[n_heads + h] * (1.0 / GATE_SOFTCAP)))
        b_col = jnp.sum(jnp.where(causal, jnp.broadcast_to(fc, (CHUNK, CHUNK)), 0.0), axis=1, keepdims=True)
        b_row = to_row(b_col)
        m_old = m_ref[h][0:1, 0:1]
        d_log = jnp.where(causal, b_col - b_row + ic, NEG_INF)
        inter_log = b_col + m_old
        m_row = jnp.maximum(inter_log, jnp.max(d_log, axis=1, keepdims=True))
        w_intra = jnp.exp(d_log - m_row)
        w_inter = jnp.exp(inter_log - m_row)
        floor = jnp.exp(-m_row)
        b_last = jnp.sum(fc, axis=1, keepdims=True)
        w_log = b_last - b_row + ic
        m_new = jnp.maximum(b_last + m_old, jnp.max(w_log, axis=1, keepdims=True))
        keep = jnp.exp(b_last + m_old - m_new)
        w_col = to_col(jnp.exp(w_log - m_new))
        m_ref[h] = jnp.broadcast_to(m_new, m_ref.shape[1:])
        return w_intra, w_inter, floor, keep, w_col

    def output_and_state(h, w_intra, w_inter, floor, keep, w_col):
        ks = slice(h * dk, (h + 1) * dk)
        vs = slice(h * dv, (h + 1) * dv)
        q = (q_ref[:, ks].astype(F32) * q_scale).astype(BF16)
        k = k_ref[:, ks]
        v_ext = jnp.concatenate([v_ref[:, vs], ones_col], axis=1)
        s = _dot_nt(q, k) * w_intra
        cn_old = cn_ref[h]
        nd = _dot(s.astype(BF16), v_ext) + w_inter * _dot(q, cn_old.astype(BF16))
        den = nd[:, dv:dv + 1]
        o_ref[:, vs] = nd[:, :dv] / jnp.maximum(jnp.abs(den), floor)
        wk = k.astype(F32) * w_col
        wk_t = jnp.concatenate([wk, jnp.zeros_like(wk)], axis=0).T
        v_pad = jnp.concatenate([v_ext, jnp.zeros_like(v_ext)], axis=0)
        cn_ref[h] = keep * cn_old + _dot(wk_t.astype(BF16), v_pad)

    weights = {}
    for step in range(n_heads + 1):
        if step < n_heads:
            weights[step] = gate_weights(step)
        if step >= 1:
            output_and_state(step - 1, *weights.pop(step - 1))


def _mlstm_scan(z, gates, *, n_batch, seq, ctx_len, dk, dv):
    t = z.shape[0]
    qk, d = H_B * dk, H_B * dv
    assert 2 * qk == d
    ncc, nlc = ctx_len // CHUNK, seq // CHUNK
    rb = functools.partial(_chunk_row_block, n_ctx_chunks=ncc, n_lat_chunks=nlc, n_batch=n_batch)
    return pl.pallas_call(
        functools.partial(_mlstm_kernel, q_scale=dk ** -0.5, n_heads=H_B, dk=dk, dv=dv),
        out_shape=jax.ShapeDtypeStruct((2, t, d), F32),
        grid=(n_batch, 2, ncc + nlc),
        in_specs=[pl.BlockSpec((CHUNK, qk), lambda b, dr, c: (rb(b, dr, c), 0)),
                  pl.BlockSpec((CHUNK, qk), lambda b, dr, c: (rb(b, dr, c), 1)),
                  pl.BlockSpec((CHUNK, d), lambda b, dr, c: (rb(b, dr, c), 1)),
                  pl.BlockSpec((None, None, 2 * H_B, 1, CHUNK), lambda b, dr, c: (dr, rb(b, dr, c), 0, 0, 0))],
        out_specs=pl.BlockSpec((None, CHUNK, d), lambda b, dr, c: (dr, rb(b, dr, c), 0)),
        scratch_shapes=[pltpu.VMEM((H_B, dk, dv + LANES), F32), pltpu.VMEM((H_B, 8, LANES), F32)],
        compiler_params=_cparams("arbitrary", "arbitrary", "arbitrary"),
        name="mlstm_scan",
    )(z, z, z, gates)


def _readout_kernel(o_ref, g_ref, ng_ref, y_ref, *, n_heads, gate):
    o = o_ref[0] + o_ref[1]
    g = g_ref[...].astype(F32)
    act = g * _sigmoid(g) if gate == "silu" else _sigmoid(g)
    dh = o.shape[1] // n_heads
    for hh in range(n_heads):
        sl = slice(hh * dh, (hh + 1) * dh)
        oh = o[:, sl]
        r = lax.rsqrt(jnp.mean(oh * oh, axis=-1, keepdims=True) + RMS_EPS)
        y_ref[:, sl] = (oh * r * ng_ref[:, sl] * act[:, sl]).astype(y_ref.dtype)


def _readout(o2, z, norm_g, *, n_heads, gate, gate_col0, rows):
    d = o2.shape[2]
    tr = _pick_tile(rows, 128, 8)
    gblk = gate_col0 // d
    return pl.pallas_call(
        functools.partial(_readout_kernel, n_heads=n_heads, gate=gate),
        out_shape=jax.ShapeDtypeStruct((rows, d), BF16),
        grid=(rows // tr,),
        in_specs=[pl.BlockSpec((2, tr, d), lambda i: (0, i, 0)),
                  pl.BlockSpec((tr, d), lambda i: (i, gblk)),
                  pl.BlockSpec((1, d), lambda i: (0, 0))],
        out_specs=pl.BlockSpec((tr, d), lambda i: (i, 0)),
        compiler_params=_cparams("parallel"),
        name="readout",
    )(o2, z, norm_g.reshape(1, d))


def _flash_kernel(*refs, n_seg):
    q_ref = refs[0]
    kv = refs[1:1 + 2 * n_seg]
    o_ref = refs[1 + 2 * n_seg]
    vx_refs = refs[2 + 2 * n_seg:]
    q = q_ref[...]
    tq = q.shape[0]
    m = jnp.full((1, tq), NEG_INF, F32)
    acc = jnp.zeros((D_V_C + ONES_ROWS, tq), F32)

    @pl.when(pl.program_id(2) == 0)
    def _():
        for sg in range(n_seg):
            vx_refs[sg][:D_V_C, :] = kv[2 * sg + 1][...]
            vx_refs[sg][D_V_C:, :] = jnp.ones((ONES_ROWS, vx_refs[sg].shape[1]), BF16)

    blocks = []
    for sg in range(n_seg):
        k_ref, vt_ref = kv[2 * sg], vx_refs[sg]
        length = k_ref.shape[0]
        tk = min(KV_TILE, length)
        blocks += [(k_ref, vt_ref, t * tk, tk) for t in range(length // tk)]

    def scores(blk):
        k_ref, _, lo, tk = blk
        return _dot_nt(k_ref[lo:lo + tk, :], q)

    st_next = scores(blocks[0])
    pending = None
    for t, (_, vt_ref, lo, tk) in enumerate(blocks):
        st = st_next
        if t + 1 < len(blocks):
            st_next = scores(blocks[t + 1])
        m_new = jnp.maximum(m, jnp.max(st, axis=0, keepdims=True))
        alpha = jnp.exp2(m - m_new)
        p = jnp.exp2(st - m_new).astype(BF16)
        m = m_new
        if pending is not None:
            a_prev, vt_prev, p_prev = pending
            acc = a_prev * acc + _dot(vt_prev, p_prev)
        pending = (alpha, vt_ref[:, lo:lo + tk], p)
    a_prev, vt_prev, p_prev = pending
    acc = a_prev * acc + _dot(vt_prev, p_prev)
    o_ref[...] = (acc[:D_V_C] / acc[D_V_C:D_V_C + 1]).T.astype(o_ref.dtype)


def _flash(qcat, kcat, vt, *, n_batch, q_rows0, q_len, segs):
    tq = _pick_tile(q_len, 512, 8)
    nq = q_len // tq
    in_specs = [pl.BlockSpec((tq, MXU_DIM), lambda b, h, i: ((q_rows0 + b * q_len) // tq + i, h))]
    args = [qcat]
    for row0, length in segs:
        in_specs.append(pl.BlockSpec((length, MXU_DIM), lambda b, h, i, row0=row0, length=length: (row0 // length + b, h)))
        in_specs.append(pl.BlockSpec((D_V_C, length), lambda b, h, i, row0=row0, length=length: (h, row0 // length + b)))
        args += [kcat, vt]
    return pl.pallas_call(
        functools.partial(_flash_kernel, n_seg=len(segs)),
        out_shape=jax.ShapeDtypeStruct((n_batch * q_len, H_C * D_V_C), BF16),
        grid=(n_batch, H_C, nq),
        in_specs=in_specs,
        out_specs=pl.BlockSpec((tq, D_V_C), lambda b, h, i: (b * nq + i, h)),
        scratch_shapes=[pltpu.VMEM((D_V_C + ONES_ROWS, length), BF16) for _, length in segs],
        compiler_params=_cparams("parallel", "parallel", "arbitrary"),
        name="flash",
    )(*args)


def _mla_prep_kernel(z_ref, gq_ref, gkv_ref, cos_ref, sin_ref, cq_ref, akv_ref, *, q_rank, kv_rank):
    cq = z_ref[:, :q_rank]
    cq_ref[...] = (cq * lax.rsqrt(jnp.mean(cq * cq, axis=-1, keepdims=True) + RMS_EPS) * gq_ref[...]).astype(BF16)
    ckv = z_ref[:, q_rank:q_rank + kv_rank]
    akv_ref[:, :kv_rank] = (ckv * lax.rsqrt(jnp.mean(ckv * ckv, axis=-1, keepdims=True) + RMS_EPS)
                            * gkv_ref[...]).astype(BF16)
    x = z_ref[:, q_rank + kv_rank:q_rank + kv_rank + LANES]
    lane = lax.broadcasted_iota(jnp.int32, x.shape, 1)
    swapped = jnp.where(lane < D_ROPE // 2, pltpu.roll(x, LANES - D_ROPE // 2, 1), pltpu.roll(x, D_ROPE // 2, 1))
    akv_ref[:, kv_rank:] = (x * cos_ref[...] + swapped * sin_ref[...]).astype(BF16)


def _mla_prep(zd, gq, gkv, cos, sin, *, q_rank, kv_rank):
    t = zd.shape[0]
    tr = _pick_tile(t, 256, 8)
    return pl.pallas_call(
        functools.partial(_mla_prep_kernel, q_rank=q_rank, kv_rank=kv_rank),
        out_shape=(jax.ShapeDtypeStruct((t, q_rank), BF16), jax.ShapeDtypeStruct((t, kv_rank + LANES), BF16)),
        grid=(t // tr,),
        in_specs=[pl.BlockSpec((tr, zd.shape[1]), lambda i: (i, 0)),
                  pl.BlockSpec((1, q_rank), lambda i: (0, 0)),
                  pl.BlockSpec((1, kv_rank), lambda i: (0, 0)),
                  pl.BlockSpec((tr, LANES), lambda i: (i, 0)),
                  pl.BlockSpec((tr, LANES), lambda i: (i, 0))],
        out_specs=(pl.BlockSpec((tr, q_rank), lambda i: (i, 0)),
                   pl.BlockSpec((tr, kv_rank + LANES), lambda i: (i, 0))),
        compiler_params=_cparams("parallel"),
        name="mla_prep",
    )(zd, gq.reshape(1, q_rank), gkv.reshape(1, kv_rank), cos, sin)


def _rope_tables(n_batch, seq, ctx_len):
    n_rows = seq // GRID_W
    rows = np.repeat(np.arange(n_rows, dtype=np.float32), GRID_W)
    cols = np.tile(np.arange(GRID_W, dtype=np.float32), n_rows)
    half = D_ROPE // 2
    inv_freq = ROPE_THETA ** (-jnp.arange(0, half, 2, dtype=F32) / half)
    ang = jnp.concatenate([jnp.asarray(rows)[:, None] * inv_freq, jnp.asarray(cols)[:, None] * inv_freq], axis=-1)
    cos, sin = jnp.cos(ang), jnp.sin(ang)
    pad1 = jnp.ones((seq, LANES - D_ROPE), F32)
    pad0 = jnp.zeros((seq, LANES - D_ROPE), F32)
    cos_l = jnp.concatenate([cos, cos, pad1], axis=1)
    sin_l = jnp.concatenate([-sin, sin, pad0], axis=1)
    cos_t = jnp.concatenate([jnp.tile(cos_l, (n_batch, 1)), jnp.ones((n_batch * ctx_len, LANES), F32)], axis=0)
    sin_t = jnp.concatenate([jnp.tile(sin_l, (n_batch, 1)), jnp.zeros((n_batch * ctx_len, LANES), F32)], axis=0)
    return cos_t, sin_t


def _gla_mixer(h, w_in_b, j, w_gdown, w_gup, b_g, norm_g, dims):
    n_batch, seq, ctx_len, d = dims
    dk, dv = d // (2 * H_A), d // H_A
    qk = H_A * dk
    z = _mm(h, w_in_b, w_lead=(j,), out_dtype=BF16)
    w_low = jnp.concatenate([w_gdown[0], w_gdown[1],
                             jnp.zeros((d, LANES - 2 * GATE_RANK_A), F32)], axis=1).astype(BF16)
    g_low = _mm(h, w_low, out_dtype=BF16)
    wup = jnp.zeros((2, LANES, qk), F32)
    wup = wup.at[0, :GATE_RANK_A].set(w_gup[0]).at[1, GATE_RANK_A:2 * GATE_RANK_A].set(w_gup[1]).astype(BF16)
    o2 = _gla_scan(z, g_low, wup, b_g.reshape(2, 1, qk), n_batch=n_batch, seq=seq, ctx_len=ctx_len, dk=dk, dv=dv)
    return _readout(o2, z, norm_g, n_heads=H_A, gate="silu", gate_col0=2 * qk + d, rows=z.shape[0])


def _mlstm_mixer(h, w_in_b, j, w_if, b_if, norm_g, dims):
    n_batch, seq, ctx_len, d = dims
    dk, dv = d // (2 * H_B), d // H_B
    qk = H_B * dk
    t = h.shape[0]
    z = _mm(h, w_in_b, w_lead=(j,), out_dtype=BF16)
    w_g = jnp.concatenate([w_if[0], w_if[1], jnp.zeros((d, LANES - 4 * H_B), F32)], axis=1).astype(BF16)
    pre = _mm(h, w_g, out_dtype=F32)[:, :4 * H_B] + jnp.concatenate([b_if[0], b_if[1]])[None, :]
    gates = pre.reshape(t // CHUNK, CHUNK, 2, 2 * H_B).transpose(2, 0, 3, 1)[:, :, :, None, :]
    o2 = _mlstm_scan(z, gates, n_batch=n_batch, seq=seq, ctx_len=ctx_len, dk=dk, dv=dv)
    return _readout(o2, z, norm_g, n_heads=H_B, gate="sigmoid", gate_col0=2 * qk + d, rows=t)


def _mla_mixer(h, w_down, q_norm_g, w_uq, kv_norm_g, w_ukv, dims, need_ctx_out):
    n_batch, seq, ctx_len, d = dims
    q_rank, kv_rank = d // 4, d // 8
    n_lat = n_batch * seq
    scale = (D_NOPE + D_ROPE) ** -0.5 * LOG2_E
    cos_t, sin_t = _rope_tables(n_batch, seq, ctx_len)

    n_down = q_rank + kv_rank + LANES
    w_down_ext = jnp.concatenate([w_down, jnp.zeros((d, n_down - w_down.shape[1]), F32)], axis=1).astype(BF16)
    zd = _mm(h, w_down_ext, out_dtype=F32)
    cqn, akv = _mla_prep(zd, q_norm_g, kv_norm_g, cos_t, sin_t, q_rank=q_rank, kv_rank=kv_rank)

    wq = w_uq.reshape(q_rank, H_C, D_NOPE + D_ROPE)
    wq = jnp.concatenate([wq, jnp.zeros((q_rank, H_C, MXU_DIM - D_NOPE - D_ROPE), F32)], axis=2)
    qcat = _mm(cqn, wq.reshape(q_rank, H_C * MXU_DIM).astype(BF16), out_dtype=BF16,
               epilogue="rope", cos=cos_t, sin=sin_t, scale=scale)

    wkv = w_ukv.reshape(kv_rank, H_C, D_NOPE + D_V_C)
    wk_top = jnp.concatenate([wkv[:, :, :D_NOPE], jnp.zeros((kv_rank, H_C, MXU_DIM - D_NOPE), F32)], axis=2)
    ident = jnp.concatenate([jnp.zeros((LANES, D_NOPE), F32), jnp.eye(LANES, dtype=F32)], axis=1)
    wk_ext = jnp.concatenate([wk_top, jnp.broadcast_to(ident[:, None, :], (LANES, H_C, MXU_DIM))], axis=0)
    kcat = _mm(akv, wk_ext.reshape(kv_rank + LANES, H_C * MXU_DIM).astype(BF16), out_dtype=BF16)
    wv_t = jnp.concatenate([wkv[:, :, D_NOPE:].reshape(kv_rank, H_C * D_V_C).T,
                            jnp.zeros((H_C * D_V_C, LANES), F32)], axis=1)
    vt = _mm(wv_t.astype(BF16), akv.T, out_dtype=BF16)

    o_lat = _flash(qcat, kcat, vt, n_batch=n_batch, q_rows0=0, q_len=seq, segs=[(0, seq), (n_lat, ctx_len)])
    if not need_ctx_out:
        return o_lat
    o_ctx = _flash(qcat, kcat, vt, n_batch=n_batch, q_rows0=n_lat, q_len=ctx_len, segs=[(n_lat, ctx_len)])
    return jnp.concatenate([o_lat, o_ctx], axis=0)


def _modulations(cond, ada_down, ada_up, ada_bias, i):
    d = cond.shape[1]
    low = _mm(cond, ada_down, w_lead=(i,), out_dtype=BF16, prologue="silu")
    m = _mm(low, ada_up, w_lead=(i,), out_dtype=F32) + ada_bias[i][None, :]
    return m.reshape(cond.shape[0], N_MOD, d)


def kernel(x, c, ctx, c_ctx, ada_down, ada_up, ada_bias, ln_g, ln_b, ffn_w_in, ffn_w_out, gla_w_in, gla_w_gdown, gla_w_gup, gla_b_g, gla_norm_g, gla_w_out, mlstm_w_in, mlstm_w_if, mlstm_b_if, mlstm_norm_g, mlstm_w_out, mla_w_down, mla_q_norm_g, mla_w_uq, mla_kv_norm_g, mla_w_ukv, mla_w_out):
    n_batch, seq, d = x.shape
    ctx_len = ctx.shape[1]
    depth = ada_down.shape[0]
    n_lat = n_batch * seq
    n_ctx = n_batch * ctx_len
    t_all = n_lat + n_ctx
    dims = (n_batch, seq, ctx_len, d)
    alpha = (2.0 * depth) ** 0.25
    n_seg = n_batch + 1
    assert seq % ELEMENTWISE_TILE_ROWS == 0 and n_ctx % math.gcd(n_ctx, ELEMENTWISE_TILE_ROWS) == 0

    def seg_of_row(row):
        return jnp.minimum(row // seq, n_batch)

    xs = jnp.concatenate([x.reshape(n_lat, d), ctx.reshape(n_ctx, d)], axis=0)
    cond = jnp.concatenate([c, c_ctx[None, :], jnp.zeros((8 - n_seg, d), F32)], axis=0)

    ffn_w_out_b, gla_w_out_b = ffn_w_out.astype(BF16), gla_w_out.astype(BF16)
    mlstm_w_out_b, mla_w_out_b = mlstm_w_out.astype(BF16), mla_w_out.astype(BF16)

    def mod_vecs(m, s, weight):
        shift = m[:n_seg, 3 * s][:, None, :]
        scale1p = 1.0 + m[:n_seg, 3 * s + 1][:, None, :]
        gate_w = weight * m[:n_seg, 3 * s + 2][:, None, :]
        return shift, scale1p, gate_w

    mods = [_modulations(cond, ada_down, ada_up, ada_bias, i) for i in range(depth)]
    shift0, scale0, _ = mod_vecs(mods[0], 0, MACARON_WEIGHT)
    hcur = _modulate(xs, scale0, shift0, seg_of_row=seg_of_row)

    def close(y_in, w, w_lead, xs, gate_w, g, b, scale1p, shift, rows):
        return _proj_ln(y_in, w, xs, gate_w, g, b, scale1p, shift, rows=rows, seg_of_row=seg_of_row,
                        seg_rows=math.gcd(seq, n_ctx), alpha=alpha, w_lead=w_lead)

    for i in range(depth):
        last = i == depth - 1
        m = mods[i]
        _, _, gw0 = mod_vecs(m, 0, MACARON_WEIGHT)
        sh1, sc1, gw1 = mod_vecs(m, 1, 1.0)
        sh2, sc2, gw2 = mod_vecs(m, 2, MACARON_WEIGHT)

        act = _ffn1(hcur, ffn_w_in, t_all, w_lead=(i, 0))
        xs, hcur = close(act, ffn_w_out_b,(i, 0), xs, gw0, ln_g[i, 0], ln_b[i, 0], sc1, sh1, t_all)

        kind, j = i % 3, i // 3
        rows1 = n_lat if last else t_all
        if kind == 0:
            y_in = _gla_mixer(hcur, gla_w_in, j, gla_w_gdown[j], gla_w_gup[j], gla_b_g[j], gla_norm_g[j], dims)
            w_o = gla_w_out_b
        elif kind == 1:
            y_in = _mlstm_mixer(hcur, mlstm_w_in, j, mlstm_w_if[j], mlstm_b_if[j], mlstm_norm_g[j], dims)
            w_o = mlstm_w_out_b
        else:
            y_in = _mla_mixer(hcur, mla_w_down[j], mla_q_norm_g[j], mla_w_uq[j], mla_kv_norm_g[j], mla_w_ukv[j],
                              dims, not last)
            w_o = mla_w_out_b
        xs, hcur = close(y_in, w_o, (j,), xs, gw1, ln_g[i, 1], ln_b[i, 1], sc2, sh2, rows1)

        if last:
            sh_n, sc_n = sh2, sc2
        else:
            sh_n, sc_n, _ = mod_vecs(mods[i + 1], 0, MACARON_WEIGHT)
        act = _ffn1(hcur, ffn_w_in, rows1, w_lead=(i, 1))
        xs, hcur = close(act, ffn_w_out_b,(i, 1), xs, gw2, ln_g[i, 2], ln_b[i, 2], sc_n, sh_n, rows1)

    return xs.reshape(n_batch, seq, d)
```

```python
import functools
import math

import numpy as np
import jax
import jax.numpy as jnp
from jax import lax
from jax.experimental import pallas as pl
from jax.experimental.pallas import tpu as pltpu

F32 = jnp.float32
BF16 = jnp.bfloat16

N_MOD = 9
MACARON_WEIGHT = 0.5
LN_EPS = 1e-5
RMS_EPS = 1e-6
CHUNK = 64
H_A = 8
GATE_RANK_A = 16
GATE_NORMALIZER_A = 16.0
H_B = 8
GATE_SOFTCAP = 15.0
M_INIT = -1e30
H_C = 32
D_NOPE = 128
D_ROPE = 64
D_V_C = 128
ROPE_THETA = 10000.0
GRID_W = 64

LANES = 128
MXU_DIM = 256
VMEM_LIMIT_BYTES = 60 * 1024 * 1024
ROW_TILE_TARGET = 512
COL_TILE_TARGET = 2048
MM_VMEM_BUDGET = 55 * 1024 * 1024
ELEMENTWISE_TILE_ROWS = 256
LN_ROWS = 32
PROJ_K_TILE = 1280
Q_TILE = 1024
KV_TILE = 512
ONES_ROWS = 16

NEG_INF = float("-inf")
LOG2_E = math.log2(math.e)


def _cparams(*sem):
    return pltpu.CompilerParams(dimension_semantics=sem, vmem_limit_bytes=VMEM_LIMIT_BYTES)


def _pick_tile(n, target, mult=LANES):
    if n <= target:
        return n
    t = (target // mult) * mult
    while t > mult and n % t:
        t -= mult
    assert n % t == 0, (n, target, mult)
    return t


def _sigmoid(x):
    return 1.0 / (1.0 + jnp.exp(-x))


def _log_sigmoid(x):
    return jnp.minimum(x, 0.0) - jnp.log1p(jnp.exp(-jnp.abs(x)))


def _dot(a, b):
    return jnp.dot(a, b, preferred_element_type=F32)


def _dot_nt(a, b):
    return lax.dot_general(a, b, (((1,), (1,)), ((), ())), preferred_element_type=F32)


def _split2(x):
    hi = x.astype(BF16)
    lo = (x - hi.astype(F32)).astype(BF16)
    return hi, lo


def _wspec(lead, block, index_map):
    return pl.BlockSpec((None,) * len(lead) + block, lambda *g: tuple(lead) + index_map(*g))


def _resident_weights(w_ref, wb_ref):
    @pl.when(pl.program_id(1) == 0)
    def _():
        def body(r, carry):
            rows = pl.ds(pl.multiple_of(r * MXU_DIM, MXU_DIM), MXU_DIM)
            wb_ref[rows, :] = w_ref[rows, :].astype(wb_ref.dtype)
            return carry
        lax.fori_loop(0, w_ref.shape[0] // MXU_DIM, body, 0)
    return wb_ref


def _mm_kernel(*refs, prologue, epilogue, scale, cast_w):
    a_ref, w_ref = refs[0], refs[1]
    if cast_w:
        w_ref = _resident_weights(w_ref, refs[-1])
        refs = refs[:-1]
    o_ref = refs[-1]
    extra = refs[2:-1]
    a = a_ref[...]
    if prologue == "silu":
        af = a.astype(F32)
        a = (af * _sigmoid(af)).astype(BF16)
    acc = _dot(a, w_ref[...])
    if epilogue == "rope":
        cos = extra[0][...]
        sin = extra[1][...]
        lane = lax.broadcasted_iota(jnp.int32, cos.shape, 1)
        first_half = lane < (D_ROPE // 2)
        for g in range(acc.shape[1] // MXU_DIM):
            lo = g * MXU_DIM
            o_ref[:, lo:lo + LANES] = (acc[:, lo:lo + LANES] * scale).astype(o_ref.dtype)
            x = acc[:, lo + LANES:lo + 2 * LANES]
            swapped = jnp.where(first_half, pltpu.roll(x, LANES - D_ROPE // 2, 1), pltpu.roll(x, D_ROPE // 2, 1))
            o_ref[:, lo + LANES:lo + 2 * LANES] = ((x * cos + swapped * sin) * scale).astype(o_ref.dtype)
    else:
        o_ref[...] = acc.astype(o_ref.dtype)


def _mm_tiles(m, kk, n, a_bytes, w_bytes, out_bytes, cast_w, tn_mult):
    tm = _pick_tile(m, ROW_TILE_TARGET, 8)
    tn_target = COL_TILE_TARGET
    while True:
        tn = _pick_tile(n, tn_target, tn_mult)
        need = 2 * tm * kk * a_bytes + 2 * kk * tn * w_bytes + 2 * tm * tn * out_bytes + tm * tn * 4
        need += kk * tn * 2 if cast_w else 0
        if need <= MM_VMEM_BUDGET or tn <= tn_mult:
            return tm, tn
        tn_target = tn - tn_mult


def _mm(a, w, *, out_dtype, w_lead=(), rows=None, prologue=None, epilogue=None, cos=None, sin=None, scale=1.0):
    m = a.shape[0] if rows is None else rows
    kk, n = w.shape[-2:]
    assert a.shape[1] == kk or kk % LANES == 0
    cast_w = w.dtype != BF16
    tm, tn = _mm_tiles(m, kk, n, a.dtype.itemsize, w.dtype.itemsize, jnp.dtype(out_dtype).itemsize, cast_w,
                       MXU_DIM if epilogue == "rope" else LANES)
    in_specs = [pl.BlockSpec((tm, kk), lambda j, i: (i, 0)),
                _wspec(w_lead, (kk, tn), lambda j, i: (0, j))]
    args = [a, w]
    if epilogue == "rope":
        in_specs += [pl.BlockSpec((tm, LANES), lambda j, i: (i, 0))] * 2
        args += [cos, sin]
    return pl.pallas_call(
        functools.partial(_mm_kernel, prologue=prologue, epilogue=epilogue, scale=scale, cast_w=cast_w),
        out_shape=jax.ShapeDtypeStruct((m, n), out_dtype),
        grid=(n // tn, m // tm),
        in_specs=in_specs,
        out_specs=pl.BlockSpec((tm, tn), lambda j, i: (i, j)),
        scratch_shapes=[pltpu.VMEM((kk, tn), BF16)] if cast_w else [],
        compiler_params=_cparams("parallel", "arbitrary"),
        name="mm_" + (prologue or "p") + "_" + (epilogue or "e"),
    )(*args)


def _ffn1_kernel(h_ref, wa_ref, wu_ref, o_ref, *scratch):
    if scratch:
        wa_ref = _resident_weights(wa_ref, scratch[0])
        wu_ref = _resident_weights(wu_ref, scratch[1])
    h = h_ref[...]
    a = _dot(h, wa_ref[...])
    u = _dot(h, wu_ref[...])
    o_ref[...] = (a * _sigmoid(a) * u).astype(o_ref.dtype)


def _ffn1(h, w_in, rows, w_lead=()):
    d, f2 = w_in.shape[-2:]
    f = f2 // 2
    cast_w = w_in.dtype != BF16
    tm = _pick_tile(rows, ROW_TILE_TARGET, 8)
    tn = _pick_tile(f, 512)
    nj = f // tn
    return pl.pallas_call(
        _ffn1_kernel,
        out_shape=jax.ShapeDtypeStruct((rows, f), BF16),
        grid=(nj, rows // tm),
        in_specs=[pl.BlockSpec((tm, d), lambda j, i: (i, 0)),
                  _wspec(w_lead, (d, tn), lambda j, i: (0, j)),
                  _wspec(w_lead, (d, tn), lambda j, i: (0, j + nj))],
        out_specs=pl.BlockSpec((tm, tn), lambda j, i: (i, j)),
        scratch_shapes=[pltpu.VMEM((d, tn), BF16)] * 2 if cast_w else [],
        compiler_params=_cparams("parallel", "arbitrary"),
        name="ffn1",
    )(h, w_in, w_in)


def _proj_ln_kernel(act_ref, w_ref, x_ref, gw_ref, lng_ref, lnb_ref, sc_ref, sh_ref, xo_ref, ho_ref,
                    acc_even_ref, acc_odd_ref, *, alpha, n_tiles):
    i = pl.program_id(0)
    k = pl.program_id(1)
    slab = xo_ref.shape[0]
    accs = (acc_even_ref, acc_odd_ref)

    def accumulate(cur_ref):
        cur_ref[...] = _dot(act_ref[...], w_ref[...]) + cur_ref[...]

    def normalise(prev_ref):
        for r in range(slab // LN_ROWS):
            lo = r * LN_ROWS
            rows = pl.ds(pl.multiple_of(k * slab + lo, LN_ROWS), LN_ROWS)
            v = alpha * x_ref[lo:lo + LN_ROWS, :] + gw_ref[...] * prev_ref[rows, :]
            mu = jnp.mean(v, axis=-1, keepdims=True)
            vc = v - mu
            var = jnp.mean(vc * vc, axis=-1, keepdims=True)
            xn = vc * lax.rsqrt(var + LN_EPS) * lng_ref[...] + lnb_ref[...]
            xo_ref[lo:lo + LN_ROWS, :] = xn
            ho_ref[lo:lo + LN_ROWS, :] = (xn * sc_ref[...] + sh_ref[...]).astype(ho_ref.dtype)
            prev_ref[rows, :] = jnp.zeros((LN_ROWS, prev_ref.shape[1]), F32)

    @pl.when(jnp.logical_and(i == 0, k == 0))
    def _():
        acc_even_ref[...] = jnp.zeros(acc_even_ref.shape, F32)
        acc_odd_ref[...] = jnp.zeros(acc_odd_ref.shape, F32)

    @pl.when(i == 0)
    def _():
        accumulate(acc_even_ref)

    for parity in range(2):
        @pl.when(jnp.logical_and(jnp.logical_and(i > 0, i < n_tiles), lax.rem(i, 2) == parity))
        def _(parity=parity):
            accumulate(accs[parity])
            normalise(accs[1 - parity])

    @pl.when(i == n_tiles)
    def _():
        normalise(accs[(n_tiles - 1) % 2])


def _proj_ln(act, w, x, gate_w, ln_g, ln_b, scale1p, shift, *, rows, seg_of_row, seg_rows, alpha, w_lead=()):
    kk, d = w.shape[-2:]
    tm = math.gcd(math.gcd(rows, ROW_TILE_TARGET), seg_rows)
    n_tiles = rows // tm
    tk = kk
    for cand in range((min(PROJ_K_TILE, kk) // MXU_DIM) * MXU_DIM, 0, -MXU_DIM):
        if kk % cand == 0 and tm % ((kk // cand) * LN_ROWS) == 0:
            tk = cand
            break
    n_k = kk // tk
    slab_rows = tm // n_k
    assert tm % LN_ROWS == 0

    def slab_index(i, k):
        return jnp.where(i == 0, 0, (i - 1) * n_k + k)

    vec = pl.BlockSpec((None, 1, d), lambda i, k: (seg_of_row(slab_index(i, k) * slab_rows), 0, 0))
    cvec = pl.BlockSpec((1, d), lambda i, k: (0, 0))
    slab = pl.BlockSpec((slab_rows, d), lambda i, k: (slab_index(i, k), 0))
    return pl.pallas_call(
        functools.partial(_proj_ln_kernel, alpha=alpha, n_tiles=n_tiles),
        out_shape=(jax.ShapeDtypeStruct((rows, d), F32), jax.ShapeDtypeStruct((rows, d), BF16)),
        grid=(n_tiles + 1, n_k),
        in_specs=[pl.BlockSpec((tm, tk), lambda i, k: (jnp.minimum(i, n_tiles - 1), k)),
                  _wspec(w_lead, (tk, d), lambda i, k: (k, 0)),
                  slab, vec, cvec, cvec, vec, vec],
        out_specs=(slab, slab),
        scratch_shapes=[pltpu.VMEM((tm, d), F32), pltpu.VMEM((tm, d), F32)],
        compiler_params=_cparams("arbitrary", "arbitrary"),
        name="proj_ln",
    )(act, w, x, gate_w, ln_g.reshape(1, d), ln_b.reshape(1, d), scale1p, shift)


def _modulate_kernel(x_ref, sc_ref, sh_ref, o_ref):
    o_ref[...] = (x_ref[...] * sc_ref[...] + sh_ref[...]).astype(o_ref.dtype)


def _modulate(x, scale1p, shift, *, seg_of_row):
    t, d = x.shape
    tr = math.gcd(t, ELEMENTWISE_TILE_ROWS)
    vec = pl.BlockSpec((None, 1, d), lambda i: (seg_of_row(i * tr), 0, 0))
    return pl.pallas_call(
        _modulate_kernel,
        out_shape=jax.ShapeDtypeStruct((t, d), BF16),
        grid=(t // tr,),
        in_specs=[pl.BlockSpec((tr, d), lambda i: (i, 0)), vec, vec],
        out_specs=pl.BlockSpec((tr, d), lambda i: (i, 0)),
        compiler_params=_cparams("parallel"),
        name="modulate",
    )(x, scale1p, shift)


def _chunk_row_block(b, dr, c, *, n_ctx_chunks, n_lat_chunks, n_batch):
    jc = jnp.where(dr == 1, n_ctx_chunks - 1 - c, c)
    jl = jnp.where(dr == 1, n_lat_chunks - 1 - (c - n_ctx_chunks), c - n_ctx_chunks)
    ctx_blk = n_batch * n_lat_chunks + b * n_ctx_chunks + jc
    lat_blk = b * n_lat_chunks + jl
    return jnp.where(c < n_ctx_chunks, ctx_blk, lat_blk)


_GLA_LEVELS = (32, 16, 8, 4, 2, 1)
_GLA_ROWS_B = len(_GLA_LEVELS) * CHUNK
_GLA_ROWS_BL = _GLA_ROWS_B + CHUNK
_GLA_ROWS_ONE = _GLA_ROWS_BL + CHUNK
_GLA_A_ROWS = _GLA_ROWS_ONE + CHUNK


def _gla_constants():
    c = CHUNK
    a = np.zeros((_GLA_A_ROWS, c), np.float32)
    masks = np.zeros((len(_GLA_LEVELS) + 1, c, c), np.float32)
    t = np.arange(c)
    for l, s in enumerate(_GLA_LEVELS):
        for i in range(c):
            r = (i // (2 * s)) * 2 * s + s - 1
            if i % (2 * s) >= s:
                a[l * c + i] = (t > r) & (t <= i)
                lo = (i // (2 * s)) * 2 * s
                masks[l, i, lo:lo + s] = 1.0
            else:
                a[l * c + i] = (t > i) & (t <= r)
    for i in range(c):
        a[_GLA_ROWS_B + i] = t <= i
        a[_GLA_ROWS_BL + i] = t > i
        a[_GLA_ROWS_ONE + i] = 1.0
    masks[len(_GLA_LEVELS)] = np.eye(c)
    a_rev = a.reshape(-1, c, c)[:, ::-1, ::-1].reshape(-1, c)
    masks_rev = masks[:, ::-1, ::-1]
    return (np.stack([a, a_rev]).astype(np.float32), np.stack([masks, masks_rev]).astype(np.float32))


def _head_readout(o, gate, norm_gain, kind):
    g = gate.astype(F32)
    act = g * _sigmoid(g) if kind == "silu" else _sigmoid(g)
    r = lax.rsqrt(jnp.mean(o * o, axis=-1, keepdims=True) + RMS_EPS)
    return (o * r * norm_gain * act).astype(BF16)


def _gla_kernel(*refs, direction, readout, q_scale, n_heads, dk, dv):
    q_ref, k_ref, v_ref, g_ref, wup_ref, bg_ref, a_ref, mask_ref = refs[:8]
    o_ref, s_ref = refs[-2:]
    if readout:
        other_ref, gate_ref, ng_ref = refs[8:11]
    dr = direction
    c = pl.program_id(1)

    @pl.when(c == 0)
    def _():
        s_ref[...] = jnp.zeros(s_ref.shape, F32)

    pre = _dot(g_ref[...], wup_ref[...]) + bg_ref[...]
    la_all = _log_sigmoid(pre) * (1.0 / GATE_NORMALIZER_A)
    amat = a_ref[...]
    row = lax.broadcasted_iota(jnp.int32, (CHUNK, 1), 0)
    if dr == 1:
        row = CHUNK - 1 - row
    reps = dv // LANES

    def exponents(h):
        hi, lo = _split2(la_all[:, h * dk:(h + 1) * dk])
        return jnp.exp(_dot(amat, hi) + _dot(amat, lo))

    def intra(h, e):
        ks = slice(h * dk, (h + 1) * dk)
        qf = q_ref[:, ks].astype(F32) * q_scale
        kf = k_ref[:, ks].astype(F32)
        p = mask_ref[len(_GLA_LEVELS)] * _dot_nt(qf.astype(BF16), kf.astype(BF16))
        for l, s in enumerate(_GLA_LEVELS):
            upper = (row & s) != 0
            m = (jnp.where(upper, qf, kf) * e[l * CHUNK:(l + 1) * CHUNK]).astype(BF16)
            p = p + mask_ref[l] * _dot_nt(m, m)
        q_hat = (qf * e[_GLA_ROWS_B:_GLA_ROWS_B + CHUNK]).astype(BF16)
        k_hat = kf * e[_GLA_ROWS_BL:_GLA_ROWS_BL + CHUNK]
        ke_t = jnp.concatenate([k_hat, e[_GLA_ROWS_ONE:_GLA_ROWS_ONE + CHUNK]], axis=0).T
        return p.astype(BF16), q_hat, ke_t

    def output_and_state(h, p, q_hat, ke_t):
        vs = slice(h * dv, (h + 1) * dv)
        v = v_ref[:, vs]
        s_old = s_ref[h]
        o = _dot(p, v) + _dot(q_hat, s_old.astype(BF16))
        if readout:
            o_ref[:, vs] = _head_readout(o + other_ref[:, vs], gate_ref[:, vs], ng_ref[:, vs], "silu")
        else:
            o_ref[:, vs] = o
        decay = jnp.concatenate([ke_t[:, CHUNK:], ke_t[:, CHUNK:]], axis=1)
        v_pad = jnp.concatenate([v, jnp.zeros_like(v)], axis=0)
        s_ref[h] = s_old * jnp.concatenate([decay] * reps, axis=1) + _dot(ke_t.astype(BF16), v_pad)

    e_of, intra_of = {}, {}
    for step in range(n_heads + 2):
        if step < n_heads:
            e_of[step] = exponents(step)
        if 0 <= step - 1 < n_heads:
            intra_of[step - 1] = intra(step - 1, e_of.pop(step - 1))
        if 0 <= step - 2 < n_heads:
            output_and_state(step - 2, *intra_of.pop(step - 2))


def _gla_scan(z, g_low, wup_ext, b_g, norm_g, *, n_batch, seq, ctx_len, dk, dv):
    t = z.shape[0]
    qk, d = H_A * dk, H_A * dv
    assert 2 * qk == d
    ncc, nlc = ctx_len // CHUNK, seq // CHUNK
    a_np, mask_np = _gla_constants()
    a_c = jnp.asarray(a_np, BF16)
    mask_c = jnp.asarray(mask_np, F32)

    def one_direction(dr, other):
        def rb(b, c):
            return _chunk_row_block(b, dr, c, n_ctx_chunks=ncc, n_lat_chunks=nlc, n_batch=n_batch)

        readout = other is not None
        in_specs = [pl.BlockSpec((CHUNK, qk), lambda b, c: (rb(b, c), 0)),
                    pl.BlockSpec((CHUNK, qk), lambda b, c: (rb(b, c), 1)),
                    pl.BlockSpec((CHUNK, d), lambda b, c: (rb(b, c), 1)),
                    pl.BlockSpec((CHUNK, LANES), lambda b, c: (rb(b, c), 0)),
                    pl.BlockSpec((None, LANES, qk), lambda b, c: (dr, 0, 0)),
                    pl.BlockSpec((None, 1, qk), lambda b, c: (dr, 0, 0)),
                    pl.BlockSpec((None, _GLA_A_ROWS, CHUNK), lambda b, c: (dr, 0, 0)),
                    pl.BlockSpec((None, len(_GLA_LEVELS) + 1, CHUNK, CHUNK), lambda b, c: (dr, 0, 0, 0))]
        args = [z, z, z, g_low, wup_ext, b_g, a_c, mask_c]
        if readout:
            in_specs += [pl.BlockSpec((CHUNK, d), lambda b, c: (rb(b, c), 0)),
                         pl.BlockSpec((CHUNK, d), lambda b, c: (rb(b, c), 2)),
                         pl.BlockSpec((1, d), lambda b, c: (0, 0))]
            args += [other, z, norm_g.reshape(1, d)]
        return pl.pallas_call(
            functools.partial(_gla_kernel, direction=dr, readout=readout, q_scale=dk ** -0.5,
                              n_heads=H_A, dk=dk, dv=dv),
            out_shape=jax.ShapeDtypeStruct((t, d), BF16 if readout else F32),
            grid=(n_batch, ncc + nlc),
            in_specs=in_specs,
            out_specs=pl.BlockSpec((CHUNK, d), lambda b, c: (rb(b, c), 0)),
            scratch_shapes=[pltpu.VMEM((H_A, dk, dv), F32)],
            compiler_params=_cparams("arbitrary", "arbitrary"),
            name="gla_scan",
        )(*args)

    return one_direction(0, one_direction(1, None))


def _mlstm_kernel(*refs, direction, readout, q_scale, n_heads, dk, dv):
    q_ref, k_ref, v_ref, gates_ref = refs[:4]
    o_ref, cn_ref, m_ref = refs[-3:]
    if readout:
        other_ref, gate_ref, ng_ref = refs[4:7]
    dr = direction
    c = pl.program_id(1)

    @pl.when(c == 0)
    def _():
        cn_ref[...] = jnp.zeros(cn_ref.shape, F32)
        m_ref[...] = jnp.full(m_ref.shape, M_INIT, F32)

    ii = lax.broadcasted_iota(jnp.int32, (CHUNK, CHUNK), 0)
    jj = lax.broadcasted_iota(jnp.int32, (CHUNK, CHUNK), 1)
    eye = ii == jj
    causal = (jj - ii) * (1 - 2 * dr) <= 0
    lane = lax.broadcasted_iota(jnp.int32, (CHUNK, LANES), 1)
    ones_col = jnp.where(lane == 0, 1.0, 0.0).astype(BF16)

    def to_col(r):
        return jnp.sum(jnp.where(eye, jnp.broadcast_to(r, (CHUNK, CHUNK)), 0.0), axis=1, keepdims=True)

    def to_row(cl):
        return jnp.sum(jnp.where(eye, jnp.broadcast_to(cl, (CHUNK, CHUNK)), 0.0), axis=0, keepdims=True)

    def gate_weights(h):
        ic = GATE_SOFTCAP * jnp.tanh(gates_ref[h] * (1.0 / GATE_SOFTCAP))
        fc = _log_sigmoid(GATE_SOFTCAP * jnp.tanh(gates_ref[n_heads + h] * (1.0 / GATE_SOFTCAP)))
        b_col = jnp.sum(jnp.where(causal, jnp.broadcast_to(fc, (CHUNK, CHUNK)), 0.0), axis=1, keepdims=True)
        b_row = to_row(b_col)
        m_old = m_ref[h][0:1, 0:1]
        d_log = jnp.where(causal, b_col - b_row + ic, NEG_INF)
        inter_log = b_col + m_old
        m_row = jnp.maximum(inter_log, jnp.max(d_log, axis=1, keepdims=True))
        w_intra = jnp.exp(d_log - m_row)
        w_inter = jnp.exp(inter_log - m_row)
        floor = jnp.exp(-m_row)
        b_last = jnp.sum(fc, axis=1, keepdims=True)
        w_log = b_last - b_row + ic
        m_new = jnp.maximum(b_last + m_old, jnp.max(w_log, axis=1, keepdims=True))
        keep = jnp.exp(b_last + m_old - m_new)
        w_col = to_col(jnp.exp(w_log - m_new))
        m_ref[h] = jnp.broadcast_to(m_new, m_ref.shape[1:])
        return w_intra, w_inter, floor, keep, w_col

    def output_and_state(h, w_intra, w_inter, floor, keep, w_col):
        ks = slice(h * dk, (h + 1) * dk)
        vs = slice(h * dv, (h + 1) * dv)
        q = (q_ref[:, ks].astype(F32) * q_scale).astype(BF16)
        k = k_ref[:, ks]
        v_ext = jnp.concatenate([v_ref[:, vs], ones_col], axis=1)
        s = _dot_nt(q, k) * w_intra
        cn_old = cn_ref[h]
        nd = _dot(s.astype(BF16), v_ext) + w_inter * _dot(q, cn_old.astype(BF16))
        den = nd[:, dv:dv + 1]
        o = nd[:, :dv] / jnp.maximum(jnp.abs(den), floor)
        if readout:
            o_ref[:, vs] = _head_readout(o + other_ref[:, vs], gate_ref[:, vs], ng_ref[:, vs], "sigmoid")
        else:
            o_ref[:, vs] = o
        wk = k.astype(F32) * w_col
        wk_t = jnp.concatenate([wk, jnp.zeros_like(wk)], axis=0).T
        v_pad = jnp.concatenate([v_ext, jnp.zeros_like(v_ext)], axis=0)
        cn_ref[h] = keep * cn_old + _dot(wk_t.astype(BF16), v_pad)

    weights = {}
    for step in range(n_heads + 1):
        if step < n_heads:
            weights[step] = gate_weights(step)
        if step >= 1:
            output_and_state(step - 1, *weights.pop(step - 1))


def _mlstm_scan(z, gates, norm_g, *, n_batch, seq, ctx_len, dk, dv):
    t = z.shape[0]
    qk, d = H_B * dk, H_B * dv
    assert 2 * qk == d
    ncc, nlc = ctx_len // CHUNK, seq // CHUNK

    def one_direction(dr, other):
        def rb(b, c):
            return _chunk_row_block(b, dr, c, n_ctx_chunks=ncc, n_lat_chunks=nlc, n_batch=n_batch)

        readout = other is not None
        in_specs = [pl.BlockSpec((CHUNK, qk), lambda b, c: (rb(b, c), 0)),
                    pl.BlockSpec((CHUNK, qk), lambda b, c: (rb(b, c), 1)),
                    pl.BlockSpec((CHUNK, d), lambda b, c: (rb(b, c), 1)),
                    pl.BlockSpec((None, None, 2 * H_B, 1, CHUNK), lambda b, c: (dr, rb(b, c), 0, 0, 0))]
        args = [z, z, z, gates]
        if readout:
            in_specs += [pl.BlockSpec((CHUNK, d), lambda b, c: (rb(b, c), 0)),
                         pl.BlockSpec((CHUNK, d), lambda b, c: (rb(b, c), 2)),
                         pl.BlockSpec((1, d), lambda b, c: (0, 0))]
            args += [other, z, norm_g.reshape(1, d)]
        return pl.pallas_call(
            functools.partial(_mlstm_kernel, direction=dr, readout=readout, q_scale=dk ** -0.5,
                              n_heads=H_B, dk=dk, dv=dv),
            out_shape=jax.ShapeDtypeStruct((t, d), BF16 if readout else F32),
            grid=(n_batch, ncc + nlc),
            in_specs=in_specs,
            out_specs=pl.BlockSpec((CHUNK, d), lambda b, c: (rb(b, c), 0)),
            scratch_shapes=[pltpu.VMEM((H_B, dk, dv + LANES), F32), pltpu.VMEM((H_B, 8, LANES), F32)],
            compiler_params=_cparams("arbitrary", "arbitrary"),
            name="mlstm_scan",
        )(*args)

    return one_direction(0, one_direction(1, None))


def _flash_kernel(*refs, n_seg):
    q_ref = refs[0]
    kv = refs[1:1 + 2 * n_seg]
    o_ref = refs[1 + 2 * n_seg]
    vx_refs = refs[2 + 2 * n_seg:]
    q = q_ref[...]
    tq = q.shape[0]
    m = jnp.full((1, tq), NEG_INF, F32)
    acc = jnp.zeros((D_V_C + ONES_ROWS, tq), F32)

    @pl.when(pl.program_id(2) == 0)
    def _():
        for sg in range(n_seg):
            vx_refs[sg][:D_V_C, :] = kv[2 * sg + 1][...]
            vx_refs[sg][D_V_C:, :] = jnp.ones((ONES_ROWS, vx_refs[sg].shape[1]), BF16)

    blocks = []
    for sg in range(n_seg):
        k_ref, vt_ref = kv[2 * sg], vx_refs[sg]
        length = k_ref.shape[0]
        tk = min(KV_TILE, length)
        blocks += [(k_ref, vt_ref, t * tk, tk) for t in range(length // tk)]

    def scores(blk):
        k_ref, _, lo, tk = blk
        return _dot_nt(k_ref[lo:lo + tk, :], q)

    st_next = scores(blocks[0])
    pending = None
    for t, (_, vt_ref, lo, tk) in enumerate(blocks):
        st = st_next
        if t + 1 < len(blocks):
            st_next = scores(blocks[t + 1])
        m_new = jnp.maximum(m, jnp.max(st, axis=0, keepdims=True))
        alpha = jnp.exp2(m - m_new)
        p = jnp.exp2(st - m_new).astype(BF16)
        m = m_new
        if pending is not None:
            a_prev, vt_prev, p_prev = pending
            acc = a_prev * acc + _dot(vt_prev, p_prev)
        pending = (alpha, vt_ref[:, lo:lo + tk], p)
    a_prev, vt_prev, p_prev = pending
    acc = a_prev * acc + _dot(vt_prev, p_prev)
    o_ref[...] = (acc[:D_V_C] / acc[D_V_C:D_V_C + 1]).T.astype(o_ref.dtype)


def _flash(qcat, kcat, vt, *, n_batch, q_rows0, q_len, segs):
    tq = _pick_tile(q_len, Q_TILE, 8)
    nq = q_len // tq
    in_specs = [pl.BlockSpec((tq, MXU_DIM), lambda b, h, i: ((q_rows0 + b * q_len) // tq + i, h))]
    args = [qcat]
    for row0, length in segs:
        in_specs.append(pl.BlockSpec((length, MXU_DIM), lambda b, h, i, row0=row0, length=length: (row0 // length + b, h)))
        in_specs.append(pl.BlockSpec((D_V_C, length), lambda b, h, i, row0=row0, length=length: (h, row0 // length + b)))
        args += [kcat, vt]
    return pl.pallas_call(
        functools.partial(_flash_kernel, n_seg=len(segs)),
        out_shape=jax.ShapeDtypeStruct((n_batch * q_len, H_C * D_V_C), BF16),
        grid=(n_batch, H_C, nq),
        in_specs=in_specs,
        out_specs=pl.BlockSpec((tq, D_V_C), lambda b, h, i: (b * nq + i, h)),
        scratch_shapes=[pltpu.VMEM((D_V_C + ONES_ROWS, length), BF16) for _, length in segs],
        compiler_params=_cparams("parallel", "parallel", "arbitrary"),
        name="flash",
    )(*args)


def _mla_prep_kernel(z_ref, gq_ref, gkv_ref, cos_ref, sin_ref, cq_ref, akv_ref, *, q_rank, kv_rank):
    cq = z_ref[:, :q_rank]
    cq_ref[...] = (cq * lax.rsqrt(jnp.mean(cq * cq, axis=-1, keepdims=True) + RMS_EPS) * gq_ref[...]).astype(BF16)
    ckv = z_ref[:, q_rank:q_rank + kv_rank]
    akv_ref[:, :kv_rank] = (ckv * lax.rsqrt(jnp.mean(ckv * ckv, axis=-1, keepdims=True) + RMS_EPS)
                            * gkv_ref[...]).astype(BF16)
    x = z_ref[:, q_rank + kv_rank:q_rank + kv_rank + LANES]
    lane = lax.broadcasted_iota(jnp.int32, x.shape, 1)
    swapped = jnp.where(lane < D_ROPE // 2, pltpu.roll(x, LANES - D_ROPE // 2, 1), pltpu.roll(x, D_ROPE // 2, 1))
    akv_ref[:, kv_rank:] = (x * cos_ref[...] + swapped * sin_ref[...]).astype(BF16)


def _mla_prep(zd, gq, gkv, cos, sin, *, q_rank, kv_rank):
    t = zd.shape[0]
    tr = _pick_tile(t, 256, 8)
    return pl.pallas_call(
        functools.partial(_mla_prep_kernel, q_rank=q_rank, kv_rank=kv_rank),
        out_shape=(jax.ShapeDtypeStruct((t, q_rank), BF16), jax.ShapeDtypeStruct((t, kv_rank + LANES), BF16)),
        grid=(t // tr,),
        in_specs=[pl.BlockSpec((tr, zd.shape[1]), lambda i: (i, 0)),
                  pl.BlockSpec((1, q_rank), lambda i: (0, 0)),
                  pl.BlockSpec((1, kv_rank), lambda i: (0, 0)),
                  pl.BlockSpec((tr, LANES), lambda i: (i, 0)),
                  pl.BlockSpec((tr, LANES), lambda i: (i, 0))],
        out_specs=(pl.BlockSpec((tr, q_rank), lambda i: (i, 0)),
                   pl.BlockSpec((tr, kv_rank + LANES), lambda i: (i, 0))),
        compiler_params=_cparams("parallel"),
        name="mla_prep",
    )(zd, gq.reshape(1, q_rank), gkv.reshape(1, kv_rank), cos, sin)


def _rope_tables(n_batch, seq, ctx_len):
    n_rows = seq // GRID_W
    rows = np.repeat(np.arange(n_rows, dtype=np.float32), GRID_W)
    cols = np.tile(np.arange(GRID_W, dtype=np.float32), n_rows)
    half = D_ROPE // 2
    inv_freq = ROPE_THETA ** (-jnp.arange(0, half, 2, dtype=F32) / half)
    ang = jnp.concatenate([jnp.asarray(rows)[:, None] * inv_freq, jnp.asarray(cols)[:, None] * inv_freq], axis=-1)
    cos, sin = jnp.cos(ang), jnp.sin(ang)
    pad1 = jnp.ones((seq, LANES - D_ROPE), F32)
    pad0 = jnp.zeros((seq, LANES - D_ROPE), F32)
    cos_l = jnp.concatenate([cos, cos, pad1], axis=1)
    sin_l = jnp.concatenate([-sin, sin, pad0], axis=1)
    cos_t = jnp.concatenate([jnp.tile(cos_l, (n_batch, 1)), jnp.ones((n_batch * ctx_len, LANES), F32)], axis=0)
    sin_t = jnp.concatenate([jnp.tile(sin_l, (n_batch, 1)), jnp.zeros((n_batch * ctx_len, LANES), F32)], axis=0)
    return cos_t, sin_t


def _gla_mixer(h, w_in_b, j, w_gdown, w_gup, b_g, norm_g, dims):
    n_batch, seq, ctx_len, d = dims
    dk, dv = d // (2 * H_A), d // H_A
    qk = H_A * dk
    z = _mm(h, w_in_b, w_lead=(j,), out_dtype=BF16)
    w_low = jnp.concatenate([w_gdown[0], w_gdown[1],
                             jnp.zeros((d, LANES - 2 * GATE_RANK_A), F32)], axis=1).astype(BF16)
    g_low = _mm(h, w_low, out_dtype=BF16)
    wup = jnp.zeros((2, LANES, qk), F32)
    wup = wup.at[0, :GATE_RANK_A].set(w_gup[0]).at[1, GATE_RANK_A:2 * GATE_RANK_A].set(w_gup[1]).astype(BF16)
    return _gla_scan(z, g_low, wup, b_g.reshape(2, 1, qk), norm_g,
                     n_batch=n_batch, seq=seq, ctx_len=ctx_len, dk=dk, dv=dv)


def _mlstm_mixer(h, w_in_b, j, w_if, b_if, norm_g, dims):
    n_batch, seq, ctx_len, d = dims
    dk, dv = d // (2 * H_B), d // H_B
    qk = H_B * dk
    t = h.shape[0]
    z = _mm(h, w_in_b, w_lead=(j,), out_dtype=BF16)
    w_g = jnp.concatenate([w_if[0], w_if[1], jnp.zeros((d, LANES - 4 * H_B), F32)], axis=1).astype(BF16)
    pre = _mm(h, w_g, out_dtype=F32)[:, :4 * H_B] + jnp.concatenate([b_if[0], b_if[1]])[None, :]
    gates = pre.reshape(t // CHUNK, CHUNK, 2, 2 * H_B).transpose(2, 0, 3, 1)[:, :, :, None, :]
    return _mlstm_scan(z, gates, norm_g, n_batch=n_batch, seq=seq, ctx_len=ctx_len, dk=dk, dv=dv)


def _mla_mixer(h, w_down, q_norm_g, w_uq, kv_norm_g, w_ukv, dims, need_ctx_out):
    n_batch, seq, ctx_len, d = dims
    q_rank, kv_rank = d // 4, d // 8
    n_lat = n_batch * seq
    scale = (D_NOPE + D_ROPE) ** -0.5 * LOG2_E
    cos_t, sin_t = _rope_tables(n_batch, seq, ctx_len)

    n_down = q_rank + kv_rank + LANES
    w_down_ext = jnp.concatenate([w_down, jnp.zeros((d, n_down - w_down.shape[1]), F32)], axis=1).astype(BF16)
    zd = _mm(h, w_down_ext, out_dtype=F32)
    cqn, akv = _mla_prep(zd, q_norm_g, kv_norm_g, cos_t, sin_t, q_rank=q_rank, kv_rank=kv_rank)

    wq = w_uq.reshape(q_rank, H_C, D_NOPE + D_ROPE)
    wq = jnp.concatenate([wq, jnp.zeros((q_rank, H_C, MXU_DIM - D_NOPE - D_ROPE), F32)], axis=2)
    qcat = _mm(cqn, wq.reshape(q_rank, H_C * MXU_DIM).astype(BF16), out_dtype=BF16,
               epilogue="rope", cos=cos_t, sin=sin_t, scale=scale)

    wkv = w_ukv.reshape(kv_rank, H_C, D_NOPE + D_V_C)
    wk_top = jnp.concatenate([wkv[:, :, :D_NOPE], jnp.zeros((kv_rank, H_C, MXU_DIM - D_NOPE), F32)], axis=2)
    ident = jnp.concatenate([jnp.zeros((LANES, D_NOPE), F32), jnp.eye(LANES, dtype=F32)], axis=1)
    wk_ext = jnp.concatenate([wk_top, jnp.broadcast_to(ident[:, None, :], (LANES, H_C, MXU_DIM))], axis=0)
    kcat = _mm(akv, wk_ext.reshape(kv_rank + LANES, H_C * MXU_DIM).astype(BF16), out_dtype=BF16)
    wv_t = jnp.concatenate([wkv[:, :, D_NOPE:].reshape(kv_rank, H_C * D_V_C).T,
                            jnp.zeros((H_C * D_V_C, LANES), F32)], axis=1)
    vt = _mm(wv_t.astype(BF16), akv.T, out_dtype=BF16)

    o_lat = _flash(qcat, kcat, vt, n_batch=n_batch, q_rows0=0, q_len=seq, segs=[(0, seq), (n_lat, ctx_len)])
    if not need_ctx_out:
        return o_lat
    o_ctx = _flash(qcat, kcat, vt, n_batch=n_batch, q_rows0=n_lat, q_len=ctx_len, segs=[(n_lat, ctx_len)])
    return jnp.concatenate([o_lat, o_ctx], axis=0)


def _modulations(cond, ada_down, ada_up, ada_bias, i):
    d = cond.shape[1]
    low = _mm(cond, ada_down, w_lead=(i,), out_dtype=BF16, prologue="silu")
    m = _mm(low, ada_up, w_lead=(i,), out_dtype=F32) + ada_bias[i][None, :]
    return m.reshape(cond.shape[0], N_MOD, d)


def kernel(x, c, ctx, c_ctx, ada_down, ada_up, ada_bias, ln_g, ln_b, ffn_w_in, ffn_w_out, gla_w_in, gla_w_gdown, gla_w_gup, gla_b_g, gla_norm_g, gla_w_out, mlstm_w_in, mlstm_w_if, mlstm_b_if, mlstm_norm_g, mlstm_w_out, mla_w_down, mla_q_norm_g, mla_w_uq, mla_kv_norm_g, mla_w_ukv, mla_w_out):
    n_batch, seq, d = x.shape
    ctx_len = ctx.shape[1]
    depth = ada_down.shape[0]
    n_lat = n_batch * seq
    n_ctx = n_batch * ctx_len
    t_all = n_lat + n_ctx
    dims = (n_batch, seq, ctx_len, d)
    alpha = (2.0 * depth) ** 0.25
    n_seg = n_batch + 1
    assert seq % ELEMENTWISE_TILE_ROWS == 0 and n_ctx % math.gcd(n_ctx, ELEMENTWISE_TILE_ROWS) == 0

    def seg_of_row(row):
        return jnp.minimum(row // seq, n_batch)

    xs = jnp.concatenate([x.reshape(n_lat, d), ctx.reshape(n_ctx, d)], axis=0)
    cond = jnp.concatenate([c, c_ctx[None, :], jnp.zeros((8 - n_seg, d), F32)], axis=0)

    ffn_w_out_b, gla_w_out_b = ffn_w_out.astype(BF16), gla_w_out.astype(BF16)
    mlstm_w_out_b, mla_w_out_b = mlstm_w_out.astype(BF16), mla_w_out.astype(BF16)

    def mod_vecs(m, s, weight):
        shift = m[:n_seg, 3 * s][:, None, :]
        scale1p = 1.0 + m[:n_seg, 3 * s + 1][:, None, :]
        gate_w = weight * m[:n_seg, 3 * s + 2][:, None, :]
        return shift, scale1p, gate_w

    mods = [_modulations(cond, ada_down, ada_up, ada_bias, i) for i in range(depth)]
    shift0, scale0, _ = mod_vecs(mods[0], 0, MACARON_WEIGHT)
    hcur = _modulate(xs, scale0, shift0, seg_of_row=seg_of_row)

    def close(y_in, w, w_lead, xs, gate_w, g, b, scale1p, shift, rows):
        return _proj_ln(y_in, w, xs, gate_w, g, b, scale1p, shift, rows=rows, seg_of_row=seg_of_row,
                        seg_rows=math.gcd(seq, n_ctx), alpha=alpha, w_lead=w_lead)

    for i in range(depth):
        last = i == depth - 1
        m = mods[i]
        _, _, gw0 = mod_vecs(m, 0, MACARON_WEIGHT)
        sh1, sc1, gw1 = mod_vecs(m, 1, 1.0)
        sh2, sc2, gw2 = mod_vecs(m, 2, MACARON_WEIGHT)

        act = _ffn1(hcur, ffn_w_in, t_all, w_lead=(i, 0))
        xs, hcur = close(act, ffn_w_out_b,(i, 0), xs, gw0, ln_g[i, 0], ln_b[i, 0], sc1, sh1, t_all)

        kind, j = i % 3, i // 3
        rows1 = n_lat if last else t_all
        if kind == 0:
            y_in = _gla_mixer(hcur, gla_w_in, j, gla_w_gdown[j], gla_w_gup[j], gla_b_g[j], gla_norm_g[j], dims)
            w_o = gla_w_out_b
        elif kind == 1:
            y_in = _mlstm_mixer(hcur, mlstm_w_in, j, mlstm_w_if[j], mlstm_b_if[j], mlstm_norm_g[j], dims)
            w_o = mlstm_w_out_b
        else:
            y_in = _mla_mixer(hcur, mla_w_down[j], mla_q_norm_g[j], mla_w_uq[j], mla_kv_norm_g[j], mla_w_ukv[j],
                              dims, not last)
            w_o = mla_w_out_b
        xs, hcur = close(y_in, w_o, (j,), xs, gw1, ln_g[i, 1], ln_b[i, 1], sc2, sh2, rows1)

        if last:
            sh_n, sc_n = sh2, sc2
        else:
            sh_n, sc_n, _ = mod_vecs(mods[i + 1], 0, MACARON_WEIGHT)
        act = _ffn1(hcur, ffn_w_in, rows1, w_lead=(i, 1))
        xs, hcur = close(act, ffn_w_out_b,(i, 1), xs, gw2, ln_g[i, 2], ln_b[i, 2], sc_n, sh_n, rows1)

    return xs.reshape(n_batch, seq, d)
```

```python
import functools
import math

import numpy as np
import jax
import jax.numpy as jnp
from jax import lax
from jax.experimental import pallas as pl
from jax.experimental.pallas import tpu as pltpu

F32 = jnp.float32
BF16 = jnp.bfloat16

N_MOD = 9
MACARON_WEIGHT = 0.5
LN_EPS = 1e-5
RMS_EPS = 1e-6
CHUNK = 64
H_A = 8
GATE_RANK_A = 16
GATE_NORMALIZER_A = 16.0
H_B = 8
GATE_SOFTCAP = 15.0
M_INIT = -1e30
H_C = 32
D_NOPE = 128
D_ROPE = 64
D_V_C = 128
ROPE_THETA = 10000.0
GRID_W = 64

LANES = 128
MXU_DIM = 256
VMEM_LIMIT_BYTES = 60 * 1024 * 1024
ROW_TILE_TARGET = 512
COL_TILE_TARGET = 2048
MM_VMEM_BUDGET = 55 * 1024 * 1024
ELEMENTWISE_TILE_ROWS = 256
LN_ROWS = 16
BF16_ROWS = 16
PROJ_K_TILE = 1280
Q_TILE = 1024
KV_TILE = 512
ONES_ROWS = 16

NEG_INF = float("-inf")
LOG2_E = math.log2(math.e)


def _cparams(*sem):
    return pltpu.CompilerParams(dimension_semantics=sem, vmem_limit_bytes=VMEM_LIMIT_BYTES)


def _pick_tile(n, target, mult=LANES):
    if n <= target:
        return n
    t = (target // mult) * mult
    while t > mult and n % t:
        t -= mult
    assert n % t == 0, (n, target, mult)
    return t


def _sigmoid(x):
    return 1.0 / (1.0 + jnp.exp(-x))


def _log_sigmoid(x):
    return jnp.minimum(x, 0.0) - jnp.log1p(jnp.exp(-jnp.abs(x)))


def _dot(a, b):
    return jnp.dot(a, b, preferred_element_type=F32)


def _dot_nt(a, b):
    return lax.dot_general(a, b, (((1,), (1,)), ((), ())), preferred_element_type=F32)


def _split2(x):
    hi = x.astype(BF16)
    lo = (x - hi.astype(F32)).astype(BF16)
    return hi, lo


def _wspec(lead, block, index_map):
    return pl.BlockSpec((None,) * len(lead) + block, lambda *g: tuple(lead) + index_map(*g))


def _resident_weights(w_ref, wb_ref):
    @pl.when(pl.program_id(1) == 0)
    def _():
        def body(r, carry):
            rows = pl.ds(pl.multiple_of(r * MXU_DIM, MXU_DIM), MXU_DIM)
            wb_ref[rows, :] = w_ref[rows, :].astype(wb_ref.dtype)
            return carry
        lax.fori_loop(0, w_ref.shape[0] // MXU_DIM, body, 0)
    return wb_ref


def _mm_kernel(*refs, prologue, epilogue, scale, cast_w):
    a_ref, w_ref = refs[0], refs[1]
    if cast_w:
        w_ref = _resident_weights(w_ref, refs[-1])
        refs = refs[:-1]
    o_ref = refs[-1]
    extra = refs[2:-1]
    a = a_ref[...]
    if prologue == "silu":
        af = a.astype(F32)
        a = (af * _sigmoid(af)).astype(BF16)
    acc = _dot(a, w_ref[...])
    if epilogue == "rope":
        cos = extra[0][...]
        sin = extra[1][...]
        lane = lax.broadcasted_iota(jnp.int32, cos.shape, 1)
        first_half = lane < (D_ROPE // 2)
        for g in range(acc.shape[1] // MXU_DIM):
            lo = g * MXU_DIM
            o_ref[:, lo:lo + LANES] = (acc[:, lo:lo + LANES] * scale).astype(o_ref.dtype)
            x = acc[:, lo + LANES:lo + 2 * LANES]
            swapped = jnp.where(first_half, pltpu.roll(x, LANES - D_ROPE // 2, 1), pltpu.roll(x, D_ROPE // 2, 1))
            o_ref[:, lo + LANES:lo + 2 * LANES] = ((x * cos + swapped * sin) * scale).astype(o_ref.dtype)
    else:
        o_ref[...] = acc.astype(o_ref.dtype)


def _mm_tiles(m, kk, n, a_bytes, w_bytes, out_bytes, cast_w, tn_mult):
    tm = _pick_tile(m, ROW_TILE_TARGET, 8)
    tn_target = COL_TILE_TARGET
    while True:
        tn = _pick_tile(n, tn_target, tn_mult)
        need = 2 * tm * kk * a_bytes + 2 * kk * tn * w_bytes + 2 * tm * tn * out_bytes + tm * tn * 4
        need += kk * tn * 2 if cast_w else 0
        if need <= MM_VMEM_BUDGET or tn <= tn_mult:
            return tm, tn
        tn_target = tn - tn_mult


def _mm(a, w, *, out_dtype, w_lead=(), rows=None, prologue=None, epilogue=None, cos=None, sin=None, scale=1.0):
    m = a.shape[0] if rows is None else rows
    kk, n = w.shape[-2:]
    assert a.shape[1] == kk or kk % LANES == 0
    cast_w = w.dtype != BF16
    tm, tn = _mm_tiles(m, kk, n, a.dtype.itemsize, w.dtype.itemsize, jnp.dtype(out_dtype).itemsize, cast_w,
                       MXU_DIM if epilogue == "rope" else LANES)
    in_specs = [pl.BlockSpec((tm, kk), lambda j, i: (i, 0)),
                _wspec(w_lead, (kk, tn), lambda j, i: (0, j))]
    args = [a, w]
    if epilogue == "rope":
        in_specs += [pl.BlockSpec((tm, LANES), lambda j, i: (i, 0))] * 2
        args += [cos, sin]
    return pl.pallas_call(
        functools.partial(_mm_kernel, prologue=prologue, epilogue=epilogue, scale=scale, cast_w=cast_w),
        out_shape=jax.ShapeDtypeStruct((m, n), out_dtype),
        grid=(n // tn, m // tm),
        in_specs=in_specs,
        out_specs=pl.BlockSpec((tm, tn), lambda j, i: (i, j)),
        scratch_shapes=[pltpu.VMEM((kk, tn), BF16)] if cast_w else [],
        compiler_params=_cparams("parallel", "arbitrary"),
        name="mm_" + (prologue or "p") + "_" + (epilogue or "e"),
    )(*args)


def _ffn1_kernel(h_ref, wa_ref, wu_ref, o_ref, *scratch):
    if scratch:
        wa_ref = _resident_weights(wa_ref, scratch[0])
        wu_ref = _resident_weights(wu_ref, scratch[1])
    h = h_ref[...]
    a = _dot(h, wa_ref[...])
    u = _dot(h, wu_ref[...])
    o_ref[...] = (a * _sigmoid(a) * u).astype(o_ref.dtype)


def _ffn1(h, w_in, rows, w_lead=()):
    d, f2 = w_in.shape[-2:]
    f = f2 // 2
    cast_w = w_in.dtype != BF16
    tm = _pick_tile(rows, ROW_TILE_TARGET, 8)
    tn = _pick_tile(f, 512)
    nj = f // tn
    return pl.pallas_call(
        _ffn1_kernel,
        out_shape=jax.ShapeDtypeStruct((rows, f), BF16),
        grid=(nj, rows // tm),
        in_specs=[pl.BlockSpec((tm, d), lambda j, i: (i, 0)),
                  _wspec(w_lead, (d, tn), lambda j, i: (0, j)),
                  _wspec(w_lead, (d, tn), lambda j, i: (0, j + nj))],
        out_specs=pl.BlockSpec((tm, tn), lambda j, i: (i, j)),
        scratch_shapes=[pltpu.VMEM((d, tn), BF16)] * 2 if cast_w else [],
        compiler_params=_cparams("parallel", "arbitrary"),
        name="ffn1",
    )(h, w_in, w_in)


def _proj_ln_kernel(act_ref, w_ref, x_ref, gw_ref, lng_ref, lnb_ref, sc_ref, sh_ref, xo_ref, ho_ref,
                    acc_even_ref, acc_odd_ref, stage_ref, *, alpha, n_tiles):
    i = pl.program_id(0)
    k = pl.program_id(1)
    slab = xo_ref.shape[0]
    accs = (acc_even_ref, acc_odd_ref)

    def accumulate(cur_ref):
        cur_ref[...] = _dot(act_ref[...], w_ref[...]) + cur_ref[...]

    def stage_slab(prev_ref):
        rows = pl.ds(pl.multiple_of(k * slab, slab), slab)
        stage_ref[...] = prev_ref[rows, :]
        prev_ref[rows, :] = jnp.zeros((slab, prev_ref.shape[1]), F32)

    def normalise_piece(r):
        lo = r * LN_ROWS
        v = alpha * x_ref[lo:lo + LN_ROWS, :] + gw_ref[...] * stage_ref[lo:lo + LN_ROWS, :]
        mu = jnp.mean(v, axis=-1, keepdims=True)
        vc = v - mu
        var = jnp.mean(vc * vc, axis=-1, keepdims=True)
        xn = vc * lax.rsqrt(var + LN_EPS) * lng_ref[...] + lnb_ref[...]
        xo_ref[lo:lo + LN_ROWS, :] = xn
        ho_ref[lo:lo + LN_ROWS, :] = (xn * sc_ref[...] + sh_ref[...]).astype(ho_ref.dtype)
        return xn

    def normalise():
        for r in range(slab // LN_ROWS):
            normalise_piece(r)

    def accumulate_beside_normalise(cur_ref):
        n_pieces = slab // LN_ROWS
        n_groups = math.gcd(n_pieces, w_ref.shape[1] // MXU_DIM)
        gcols = w_ref.shape[1] // n_groups
        per_group = n_pieces // n_groups
        anchor = None
        for c in range(n_groups):
            cols = slice(c * gcols, (c + 1) * gcols)
            w = w_ref[:, cols]
            if anchor is not None:
                head = jnp.concatenate([w[:BF16_ROWS, :LANES] + anchor, w[:BF16_ROWS, LANES:]], axis=1)
                w = jnp.concatenate([head, w[BF16_ROWS:, :]], axis=0)
            cur_ref[:, cols] = _dot(act_ref[...], w) + cur_ref[:, cols]
            for r in range(c * per_group, (c + 1) * per_group):
                xn = normalise_piece(r)
            anchor = jnp.minimum(jnp.abs(xn[:BF16_ROWS, :LANES]), 0.0).astype(w_ref.dtype)

    @pl.when(jnp.logical_and(i == 0, k == 0))
    def _():
        acc_even_ref[...] = jnp.zeros(acc_even_ref.shape, F32)
        acc_odd_ref[...] = jnp.zeros(acc_odd_ref.shape, F32)

    @pl.when(i == 0)
    def _():
        accumulate(acc_even_ref)

    for parity in range(2):
        @pl.when(jnp.logical_and(jnp.logical_and(i > 0, i < n_tiles), lax.rem(i, 2) == parity))
        def _(parity=parity):
            stage_slab(accs[1 - parity])
            accumulate_beside_normalise(accs[parity])

    @pl.when(i == n_tiles)
    def _():
        stage_slab(accs[(n_tiles - 1) % 2])
        normalise()


def _proj_ln(act, w, x, gate_w, ln_g, ln_b, scale1p, shift, *, rows, seg_of_row, seg_rows, alpha, w_lead=()):
    kk, d = w.shape[-2:]
    tm = math.gcd(math.gcd(rows, ROW_TILE_TARGET), seg_rows)
    n_tiles = rows // tm
    tk = kk
    for cand in range((min(PROJ_K_TILE, kk) // MXU_DIM) * MXU_DIM, 0, -MXU_DIM):
        if kk % cand == 0 and tm % ((kk // cand) * LN_ROWS) == 0:
            tk = cand
            break
    n_k = kk // tk
    slab_rows = tm // n_k
    assert tm % LN_ROWS == 0

    def slab_index(i, k):
        return jnp.where(i == 0, 0, (i - 1) * n_k + k)

    vec = pl.BlockSpec((None, 1, d), lambda i, k: (seg_of_row(slab_index(i, k) * slab_rows), 0, 0))
    cvec = pl.BlockSpec((1, d), lambda i, k: (0, 0))
    slab = pl.BlockSpec((slab_rows, d), lambda i, k: (slab_index(i, k), 0))
    return pl.pallas_call(
        functools.partial(_proj_ln_kernel, alpha=alpha, n_tiles=n_tiles),
        out_shape=(jax.ShapeDtypeStruct((rows, d), F32), jax.ShapeDtypeStruct((rows, d), BF16)),
        grid=(n_tiles + 1, n_k),
        in_specs=[pl.BlockSpec((tm, tk), lambda i, k: (jnp.minimum(i, n_tiles - 1), k)),
                  _wspec(w_lead, (tk, d), lambda i, k: (k, 0)),
                  slab, vec, cvec, cvec, vec, vec],
        out_specs=(slab, slab),
        scratch_shapes=[pltpu.VMEM((tm, d), F32), pltpu.VMEM((tm, d), F32), pltpu.VMEM((slab_rows, d), F32)],
        compiler_params=_cparams("arbitrary", "arbitrary"),
        name="proj_ln",
    )(act, w, x, gate_w, ln_g.reshape(1, d), ln_b.reshape(1, d), scale1p, shift)


def _modulate_kernel(x_ref, sc_ref, sh_ref, o_ref):
    o_ref[...] = (x_ref[...] * sc_ref[...] + sh_ref[...]).astype(o_ref.dtype)


def _modulate(x, scale1p, shift, *, seg_of_row):
    t, d = x.shape
    tr = math.gcd(t, ELEMENTWISE_TILE_ROWS)
    vec = pl.BlockSpec((None, 1, d), lambda i: (seg_of_row(i * tr), 0, 0))
    return pl.pallas_call(
        _modulate_kernel,
        out_shape=jax.ShapeDtypeStruct((t, d), BF16),
        grid=(t // tr,),
        in_specs=[pl.BlockSpec((tr, d), lambda i: (i, 0)), vec, vec],
        out_specs=pl.BlockSpec((tr, d), lambda i: (i, 0)),
        compiler_params=_cparams("parallel"),
        name="modulate",
    )(x, scale1p, shift)


def _chunk_row_block(b, dr, c, *, n_ctx_chunks, n_lat_chunks, n_batch):
    jc = jnp.where(dr == 1, n_ctx_chunks - 1 - c, c)
    jl = jnp.where(dr == 1, n_lat_chunks - 1 - (c - n_ctx_chunks), c - n_ctx_chunks)
    ctx_blk = n_batch * n_lat_chunks + b * n_ctx_chunks + jc
    lat_blk = b * n_lat_chunks + jl
    return jnp.where(c < n_ctx_chunks, ctx_blk, lat_blk)


_GLA_LEVELS = (32, 16, 8, 4, 2, 1)
_GLA_ROWS_B = len(_GLA_LEVELS) * CHUNK
_GLA_ROWS_BL = _GLA_ROWS_B + CHUNK
_GLA_ROWS_ONE = _GLA_ROWS_BL + CHUNK
_GLA_A_ROWS = _GLA_ROWS_ONE + CHUNK


def _gla_constants():
    c = CHUNK
    a = np.zeros((_GLA_A_ROWS, c), np.float32)
    masks = np.zeros((len(_GLA_LEVELS) + 1, c, c), np.float32)
    t = np.arange(c)
    for l, s in enumerate(_GLA_LEVELS):
        for i in range(c):
            r = (i // (2 * s)) * 2 * s + s - 1
            if i % (2 * s) >= s:
                a[l * c + i] = (t > r) & (t <= i)
                lo = (i // (2 * s)) * 2 * s
                masks[l, i, lo:lo + s] = 1.0
            else:
                a[l * c + i] = (t > i) & (t <= r)
    for i in range(c):
        a[_GLA_ROWS_B + i] = t <= i
        a[_GLA_ROWS_BL + i] = t > i
        a[_GLA_ROWS_ONE + i] = 1.0
    masks[len(_GLA_LEVELS)] = np.eye(c)
    a_rev = a.reshape(-1, c, c)[:, ::-1, ::-1].reshape(-1, c)
    masks_rev = masks[:, ::-1, ::-1]
    return (np.stack([a, a_rev]).astype(np.float32), np.stack([masks, masks_rev]).astype(np.float32))


def _head_readout(o, gate, norm_gain, kind):
    g = gate.astype(F32)
    act = g * _sigmoid(g) if kind == "silu" else _sigmoid(g)
    r = lax.rsqrt(jnp.mean(o * o, axis=-1, keepdims=True) + RMS_EPS)
    return (o * r * norm_gain * act).astype(BF16)


def _gla_kernel(*refs, direction, readout, q_scale, n_heads, dk, dv):
    q_ref, k_ref, v_ref, g_ref, wup_ref, bg_ref, a_ref, mask_ref = refs[:8]
    o_ref, s_ref = refs[-2:]
    if readout:
        other_ref, gate_ref, ng_ref = refs[8:11]
    dr = direction
    c = pl.program_id(1)

    @pl.when(c == 0)
    def _():
        s_ref[...] = jnp.zeros(s_ref.shape, F32)

    pre = _dot(g_ref[...], wup_ref[...]) + bg_ref[...]
    la_all = _log_sigmoid(pre) * (1.0 / GATE_NORMALIZER_A)
    amat = a_ref[...]
    row = lax.broadcasted_iota(jnp.int32, (CHUNK, 1), 0)
    if dr == 1:
        row = CHUNK - 1 - row
    reps = dv // LANES

    def exponents(h):
        hi, lo = _split2(la_all[:, h * dk:(h + 1) * dk])
        return jnp.exp(_dot(amat, hi) + _dot(amat, lo))

    def intra(h, e):
        ks = slice(h * dk, (h + 1) * dk)
        qf = q_ref[:, ks].astype(F32) * q_scale
        kf = k_ref[:, ks].astype(F32)
        p = mask_ref[len(_GLA_LEVELS)] * _dot_nt(qf.astype(BF16), kf.astype(BF16))
        for l, s in enumerate(_GLA_LEVELS):
            upper = (row & s) != 0
            m = (jnp.where(upper, qf, kf) * e[l * CHUNK:(l + 1) * CHUNK]).astype(BF16)
            p = p + mask_ref[l] * _dot_nt(m, m)
        q_hat = (qf * e[_GLA_ROWS_B:_GLA_ROWS_B + CHUNK]).astype(BF16)
        k_hat = kf * e[_GLA_ROWS_BL:_GLA_ROWS_BL + CHUNK]
        ke_t = jnp.concatenate([k_hat, e[_GLA_ROWS_ONE:_GLA_ROWS_ONE + CHUNK]], axis=0).T
        return p.astype(BF16), q_hat, ke_t

    def output_and_state(h, p, q_hat, ke_t):
        vs = slice(h * dv, (h + 1) * dv)
        v = v_ref[:, vs]
        s_old = s_ref[h]
        o = _dot(p, v) + _dot(q_hat, s_old.astype(BF16))
        if readout:
            o_ref[:, vs] = _head_readout(o + other_ref[:, vs], gate_ref[:, vs], ng_ref[:, vs], "silu")
        else:
            o_ref[:, vs] = o
        decay = jnp.concatenate([ke_t[:, CHUNK:], ke_t[:, CHUNK:]], axis=1)
        v_pad = jnp.concatenate([v, jnp.zeros_like(v)], axis=0)
        s_ref[h] = s_old * jnp.concatenate([decay] * reps, axis=1) + _dot(ke_t.astype(BF16), v_pad)

    e_of, intra_of = {}, {}
    for step in range(n_heads + 2):
        if step < n_heads:
            e_of[step] = exponents(step)
        if 0 <= step - 1 < n_heads:
            intra_of[step - 1] = intra(step - 1, e_of.pop(step - 1))
        if 0 <= step - 2 < n_heads:
            output_and_state(step - 2, *intra_of.pop(step - 2))


def _gla_scan(z, g_low, wup_ext, b_g, norm_g, *, n_batch, seq, ctx_len, dk, dv):
    t = z.shape[0]
    qk, d = H_A * dk, H_A * dv
    assert 2 * qk == d
    ncc, nlc = ctx_len // CHUNK, seq // CHUNK
    a_np, mask_np = _gla_constants()
    a_c = jnp.asarray(a_np, BF16)
    mask_c = jnp.asarray(mask_np, F32)

    def one_direction(dr, other):
        def rb(b, c):
            return _chunk_row_block(b, dr, c, n_ctx_chunks=ncc, n_lat_chunks=nlc, n_batch=n_batch)

        readout = other is not None
        in_specs = [pl.BlockSpec((CHUNK, qk), lambda b, c: (rb(b, c), 0)),
                    pl.BlockSpec((CHUNK, qk), lambda b, c: (rb(b, c), 1)),
                    pl.BlockSpec((CHUNK, d), lambda b, c: (rb(b, c), 1)),
                    pl.BlockSpec((CHUNK, LANES), lambda b, c: (rb(b, c), 0)),
                    pl.BlockSpec((None, LANES, qk), lambda b, c: (dr, 0, 0)),
                    pl.BlockSpec((None, 1, qk), lambda b, c: (dr, 0, 0)),
                    pl.BlockSpec((None, _GLA_A_ROWS, CHUNK), lambda b, c: (dr, 0, 0)),
                    pl.BlockSpec((None, len(_GLA_LEVELS) + 1, CHUNK, CHUNK), lambda b, c: (dr, 0, 0, 0))]
        args = [z, z, z, g_low, wup_ext, b_g, a_c, mask_c]
        if readout:
            in_specs += [pl.BlockSpec((CHUNK, d), lambda b, c: (rb(b, c), 0)),
                         pl.BlockSpec((CHUNK, d), lambda b, c: (rb(b, c), 2)),
                         pl.BlockSpec((1, d), lambda b, c: (0, 0))]
            args += [other, z, norm_g.reshape(1, d)]
        return pl.pallas_call(
            functools.partial(_gla_kernel, direction=dr, readout=readout, q_scale=dk ** -0.5,
                              n_heads=H_A, dk=dk, dv=dv),
            out_shape=jax.ShapeDtypeStruct((t, d), BF16 if readout else F32),
            grid=(n_batch, ncc + nlc),
            in_specs=in_specs,
            out_specs=pl.BlockSpec((CHUNK, d), lambda b, c: (rb(b, c), 0)),
            scratch_shapes=[pltpu.VMEM((H_A, dk, dv), F32)],
            compiler_params=_cparams("arbitrary", "arbitrary"),
            name="gla_scan",
        )(*args)

    return one_direction(0, one_direction(1, None))


def _mlstm_kernel(*refs, direction, readout, q_scale, n_heads, dk, dv):
    q_ref, k_ref, v_ref, gates_ref = refs[:4]
    o_ref, cn_ref, m_ref = refs[-3:]
    if readout:
        other_ref, gate_ref, ng_ref = refs[4:7]
    dr = direction
    c = pl.program_id(1)

    @pl.when(c == 0)
    def _():
        cn_ref[...] = jnp.zeros(cn_ref.shape, F32)
        m_ref[...] = jnp.full(m_ref.shape, M_INIT, F32)

    ii = lax.broadcasted_iota(jnp.int32, (CHUNK, CHUNK), 0)
    jj = lax.broadcasted_iota(jnp.int32, (CHUNK, CHUNK), 1)
    eye = ii == jj
    causal = (jj - ii) * (1 - 2 * dr) <= 0
    lane = lax.broadcasted_iota(jnp.int32, (CHUNK, LANES), 1)
    ones_col = jnp.where(lane == 0, 1.0, 0.0).astype(BF16)

    def to_col(r):
        return jnp.sum(jnp.where(eye, jnp.broadcast_to(r, (CHUNK, CHUNK)), 0.0), axis=1, keepdims=True)

    def to_row(cl):
        return jnp.sum(jnp.where(eye, jnp.broadcast_to(cl, (CHUNK, CHUNK)), 0.0), axis=0, keepdims=True)

    def gate_weights(h):
        ic = GATE_SOFTCAP * jnp.tanh(gates_ref[h] * (1.0 / GATE_SOFTCAP))
        fc = _log_sigmoid(GATE_SOFTCAP * jnp.tanh(gates_ref[n_heads + h] * (1.0 / GATE_SOFTCAP)))
        b_col = jnp.sum(jnp.where(causal, jnp.broadcast_to(fc, (CHUNK, CHUNK)), 0.0), axis=1, keepdims=True)
        b_row = to_row(b_col)
        m_old = m_ref[h][0:1, 0:1]
        d_log = jnp.where(causal, b_col - b_row + ic, NEG_INF)
        inter_log = b_col + m_old
        m_row = jnp.maximum(inter_log, jnp.max(d_log, axis=1, keepdims=True))
        w_intra = jnp.exp(d_log - m_row)
        w_inter = jnp.exp(inter_log - m_row)
        floor = jnp.exp(-m_row)
        b_last = jnp.sum(fc, axis=1, keepdims=True)
        w_log = b_last - b_row + ic
        m_new = jnp.maximum(b_last + m_old, jnp.max(w_log, axis=1, keepdims=True))
        keep = jnp.exp(b_last + m_old - m_new)
        w_col = to_col(jnp.exp(w_log - m_new))
        m_ref[h] = jnp.broadcast_to(m_new, m_ref.shape[1:])
        return w_intra, w_inter, floor, keep, w_col

    def output_and_state(h, w_intra, w_inter, floor, keep, w_col):
        ks = slice(h * dk, (h + 1) * dk)
        vs = slice(h * dv, (h + 1) * dv)
        q = (q_ref[:, ks].astype(F32) * q_scale).astype(BF16)
        k = k_ref[:, ks]
        v_ext = jnp.concatenate([v_ref[:, vs], ones_col], axis=1)
        s = _dot_nt(q, k) * w_intra
        cn_old = cn_ref[h]
        nd = _dot(s.astype(BF16), v_ext) + w_inter * _dot(q, cn_old.astype(BF16))
        den = nd[:, dv:dv + 1]
        o = nd[:, :dv] / jnp.maximum(jnp.abs(den), floor)
        if readout:
            o_ref[:, vs] = _head_readout(o + other_ref[:, vs], gate_ref[:, vs], ng_ref[:, vs], "sigmoid")
        else:
            o_ref[:, vs] = o
        wk = k.astype(F32) * w_col
        wk_t = jnp.concatenate([wk, jnp.zeros_like(wk)], axis=0).T
        v_pad = jnp.concatenate([v_ext, jnp.zeros_like(v_ext)], axis=0)
        cn_ref[h] = keep * cn_old + _dot(wk_t.astype(BF16), v_pad)

    weights = {}
    for step in range(n_heads + 1):
        if step < n_heads:
            weights[step] = gate_weights(step)
        if step >= 1:
            output_and_state(step - 1, *weights.pop(step - 1))


def _mlstm_scan(z, gates, norm_g, *, n_batch, seq, ctx_len, dk, dv):
    t = z.shape[0]
    qk, d = H_B * dk, H_B * dv
    assert 2 * qk == d
    ncc, nlc = ctx_len // CHUNK, seq // CHUNK

    def one_direction(dr, other):
        def rb(b, c):
            return _chunk_row_block(b, dr, c, n_ctx_chunks=ncc, n_lat_chunks=nlc, n_batch=n_batch)

        readout = other is not None
        in_specs = [pl.BlockSpec((CHUNK, qk), lambda b, c: (rb(b, c), 0)),
                    pl.BlockSpec((CHUNK, qk), lambda b, c: (rb(b, c), 1)),
                    pl.BlockSpec((CHUNK, d), lambda b, c: (rb(b, c), 1)),
                    pl.BlockSpec((None, None, 2 * H_B, 1, CHUNK), lambda b, c: (dr, rb(b, c), 0, 0, 0))]
        args = [z, z, z, gates]
        if readout:
            in_specs += [pl.BlockSpec((CHUNK, d), lambda b, c: (rb(b, c), 0)),
                         pl.BlockSpec((CHUNK, d), lambda b, c: (rb(b, c), 2)),
                         pl.BlockSpec((1, d), lambda b, c: (0, 0))]
            args += [other, z, norm_g.reshape(1, d)]
        return pl.pallas_call(
            functools.partial(_mlstm_kernel, direction=dr, readout=readout, q_scale=dk ** -0.5,
                              n_heads=H_B, dk=dk, dv=dv),
            out_shape=jax.ShapeDtypeStruct((t, d), BF16 if readout else F32),
            grid=(n_batch, ncc + nlc),
            in_specs=in_specs,
            out_specs=pl.BlockSpec((CHUNK, d), lambda b, c: (rb(b, c), 0)),
            scratch_shapes=[pltpu.VMEM((H_B, dk, dv + LANES), F32), pltpu.VMEM((H_B, 8, LANES), F32)],
            compiler_params=_cparams("arbitrary", "arbitrary"),
            name="mlstm_scan",
        )(*args)

    return one_direction(0, one_direction(1, None))


def _flash_kernel(*refs, n_seg):
    q_ref = refs[0]
    kv = refs[1:1 + 2 * n_seg]
    o_ref = refs[1 + 2 * n_seg]
    vx_refs = refs[2 + 2 * n_seg:]
    q = q_ref[...]
    tq = q.shape[0]
    m = jnp.full((1, tq), NEG_INF, F32)
    acc = jnp.zeros((D_V_C + ONES_ROWS, tq), F32)

    @pl.when(pl.program_id(2) == 0)
    def _():
        for sg in range(n_seg):
            vx_refs[sg][:D_V_C, :] = kv[2 * sg + 1][...]
            vx_refs[sg][D_V_C:, :] = jnp.ones((ONES_ROWS, vx_refs[sg].shape[1]), BF16)

    blocks = []
    for sg in range(n_seg):
        k_ref, vt_ref = kv[2 * sg], vx_refs[sg]
        length = k_ref.shape[0]
        tk = min(KV_TILE, length)
        blocks += [(k_ref, vt_ref, t * tk, tk) for t in range(length // tk)]

    def scores(blk):
        k_ref, _, lo, tk = blk
        return _dot_nt(k_ref[lo:lo + tk, :], q)

    st_next = scores(blocks[0])
    pending = None
    for t, (_, vt_ref, lo, tk) in enumerate(blocks):
        st = st_next
        if t + 1 < len(blocks):
            st_next = scores(blocks[t + 1])
        m_new = jnp.maximum(m, jnp.max(st, axis=0, keepdims=True))
        alpha = jnp.exp2(m - m_new)
        p = jnp.exp2(st - m_new).astype(BF16)
        m = m_new
        if pending is not None:
            a_prev, vt_prev, p_prev = pending
            acc = a_prev * acc + _dot(vt_prev, p_prev)
        pending = (alpha, vt_ref[:, lo:lo + tk], p)
    a_prev, vt_prev, p_prev = pending
    acc = a_prev * acc + _dot(vt_prev, p_prev)
    o_ref[...] = (acc[:D_V_C] / acc[D_V_C:D_V_C + 1]).T.astype(o_ref.dtype)


def _flash(qcat, kcat, vt, *, n_batch, q_rows0, q_len, segs):
    tq = _pick_tile(q_len, Q_TILE, 8)
    nq = q_len // tq
    in_specs = [pl.BlockSpec((tq, MXU_DIM), lambda b, h, i: ((q_rows0 + b * q_len) // tq + i, h))]
    args = [qcat]
    for row0, length in segs:
        in_specs.append(pl.BlockSpec((length, MXU_DIM), lambda b, h, i, row0=row0, length=length: (row0 // length + b, h)))
        in_specs.append(pl.BlockSpec((D_V_C, length), lambda b, h, i, row0=row0, length=length: (h, row0 // length + b)))
        args += [kcat, vt]
    return pl.pallas_call(
        functools.partial(_flash_kernel, n_seg=len(segs)),
        out_shape=jax.ShapeDtypeStruct((n_batch * q_len, H_C * D_V_C), BF16),
        grid=(n_batch, H_C, nq),
        in_specs=in_specs,
        out_specs=pl.BlockSpec((tq, D_V_C), lambda b, h, i: (b * nq + i, h)),
        scratch_shapes=[pltpu.VMEM((D_V_C + ONES_ROWS, length), BF16) for _, length in segs],
        compiler_params=_cparams("parallel", "parallel", "arbitrary"),
        name="flash",
    )(*args)


def _mla_prep_kernel(z_ref, gq_ref, gkv_ref, cos_ref, sin_ref, cq_ref, akv_ref, *, q_rank, kv_rank):
    cq = z_ref[:, :q_rank]
    cq_ref[...] = (cq * lax.rsqrt(jnp.mean(cq * cq, axis=-1, keepdims=True) + RMS_EPS) * gq_ref[...]).astype(BF16)
    ckv = z_ref[:, q_rank:q_rank + kv_rank]
    akv_ref[:, :kv_rank] = (ckv * lax.rsqrt(jnp.mean(ckv * ckv, axis=-1, keepdims=True) + RMS_EPS)
                            * gkv_ref[...]).astype(BF16)
    x = z_ref[:, q_rank + kv_rank:q_rank + kv_rank + LANES]
    lane = lax.broadcasted_iota(jnp.int32, x.shape, 1)
    swapped = jnp.where(lane < D_ROPE // 2, pltpu.roll(x, LANES - D_ROPE // 2, 1), pltpu.roll(x, D_ROPE // 2, 1))
    akv_ref[:, kv_rank:] = (x * cos_ref[...] + swapped * sin_ref[...]).astype(BF16)


def _mla_prep(zd, gq, gkv, cos, sin, *, q_rank, kv_rank):
    t = zd.shape[0]
    tr = _pick_tile(t, 256, 8)
    return pl.pallas_call(
        functools.partial(_mla_prep_kernel, q_rank=q_rank, kv_rank=kv_rank),
        out_shape=(jax.ShapeDtypeStruct((t, q_rank), BF16), jax.ShapeDtypeStruct((t, kv_rank + LANES), BF16)),
        grid=(t // tr,),
        in_specs=[pl.BlockSpec((tr, zd.shape[1]), lambda i: (i, 0)),
                  pl.BlockSpec((1, q_rank), lambda i: (0, 0)),
                  pl.BlockSpec((1, kv_rank), lambda i: (0, 0)),
                  pl.BlockSpec((tr, LANES), lambda i: (i, 0)),
                  pl.BlockSpec((tr, LANES), lambda i: (i, 0))],
        out_specs=(pl.BlockSpec((tr, q_rank), lambda i: (i, 0)),
                   pl.BlockSpec((tr, kv_rank + LANES), lambda i: (i, 0))),
        compiler_params=_cparams("parallel"),
        name="mla_prep",
    )(zd, gq.reshape(1, q_rank), gkv.reshape(1, kv_rank), cos, sin)


def _rope_tables(n_batch, seq, ctx_len):
    n_rows = seq // GRID_W
    rows = np.repeat(np.arange(n_rows, dtype=np.float32), GRID_W)
    cols = np.tile(np.arange(GRID_W, dtype=np.float32), n_rows)
    half = D_ROPE // 2
    inv_freq = ROPE_THETA ** (-jnp.arange(0, half, 2, dtype=F32) / half)
    ang = jnp.concatenate([jnp.asarray(rows)[:, None] * inv_freq, jnp.asarray(cols)[:, None] * inv_freq], axis=-1)
    cos, sin = jnp.cos(ang), jnp.sin(ang)
    pad1 = jnp.ones((seq, LANES - D_ROPE), F32)
    pad0 = jnp.zeros((seq, LANES - D_ROPE), F32)
    cos_l = jnp.concatenate([cos, cos, pad1], axis=1)
    sin_l = jnp.concatenate([-sin, sin, pad0], axis=1)
    cos_t = jnp.concatenate([jnp.tile(cos_l, (n_batch, 1)), jnp.ones((n_batch * ctx_len, LANES), F32)], axis=0)
    sin_t = jnp.concatenate([jnp.tile(sin_l, (n_batch, 1)), jnp.zeros((n_batch * ctx_len, LANES), F32)], axis=0)
    return cos_t, sin_t


def _gla_mixer(h, w_in_b, j, w_gdown, w_gup, b_g, norm_g, dims):
    n_batch, seq, ctx_len, d = dims
    dk, dv = d // (2 * H_A), d // H_A
    qk = H_A * dk
    z = _mm(h, w_in_b, w_lead=(j,), out_dtype=BF16)
    w_low = jnp.concatenate([w_gdown[0], w_gdown[1],
                             jnp.zeros((d, LANES - 2 * GATE_RANK_A), F32)], axis=1).astype(BF16)
    g_low = _mm(h, w_low, out_dtype=BF16)
    wup = jnp.zeros((2, LANES, qk), F32)
    wup = wup.at[0, :GATE_RANK_A].set(w_gup[0]).at[1, GATE_RANK_A:2 * GATE_RANK_A].set(w_gup[1]).astype(BF16)
    return _gla_scan(z, g_low, wup, b_g.reshape(2, 1, qk), norm_g,
                     n_batch=n_batch, seq=seq, ctx_len=ctx_len, dk=dk, dv=dv)


def _mlstm_mixer(h, w_in_b, j, w_if, b_if, norm_g, dims):
    n_batch, seq, ctx_len, d = dims
    dk, dv = d // (2 * H_B), d // H_B
    qk = H_B * dk
    t = h.shape[0]
    z = _mm(h, w_in_b, w_lead=(j,), out_dtype=BF16)
    w_g = jnp.concatenate([w_if[0], w_if[1], jnp.zeros((d, LANES - 4 * H_B), F32)], axis=1).astype(BF16)
    pre = _mm(h, w_g, out_dtype=F32)[:, :4 * H_B] + jnp.concatenate([b_if[0], b_if[1]])[None, :]
    gates = pre.reshape(t // CHUNK, CHUNK, 2, 2 * H_B).transpose(2, 0, 3, 1)[:, :, :, None, :]
    return _mlstm_scan(z, gates, norm_g, n_batch=n_batch, seq=seq, ctx_len=ctx_len, dk=dk, dv=dv)


def _mla_mixer(h, w_down, q_norm_g, w_uq, kv_norm_g, w_ukv, dims, need_ctx_out):
    n_batch, seq, ctx_len, d = dims
    q_rank, kv_rank = d // 4, d // 8
    n_lat = n_batch * seq
    scale = (D_NOPE + D_ROPE) ** -0.5 * LOG2_E
    cos_t, sin_t = _rope_tables(n_batch, seq, ctx_len)

    n_down = q_rank + kv_rank + LANES
    w_down_ext = jnp.concatenate([w_down, jnp.zeros((d, n_down - w_down.shape[1]), F32)], axis=1).astype(BF16)
    zd = _mm(h, w_down_ext, out_dtype=F32)
    cqn, akv = _mla_prep(zd, q_norm_g, kv_norm_g, cos_t, sin_t, q_rank=q_rank, kv_rank=kv_rank)

    wq = w_uq.reshape(q_rank, H_C, D_NOPE + D_ROPE)
    wq = jnp.concatenate([wq, jnp.zeros((q_rank, H_C, MXU_DIM - D_NOPE - D_ROPE), F32)], axis=2)
    qcat = _mm(cqn, wq.reshape(q_rank, H_C * MXU_DIM).astype(BF16), out_dtype=BF16,
               epilogue="rope", cos=cos_t, sin=sin_t, scale=scale)

    wkv = w_ukv.reshape(kv_rank, H_C, D_NOPE + D_V_C)
    wk_top = jnp.concatenate([wkv[:, :, :D_NOPE], jnp.zeros((kv_rank, H_C, MXU_DIM - D_NOPE), F32)], axis=2)
    ident = jnp.concatenate([jnp.zeros((LANES, D_NOPE), F32), jnp.eye(LANES, dtype=F32)], axis=1)
    wk_ext = jnp.concatenate([wk_top, jnp.broadcast_to(ident[:, None, :], (LANES, H_C, MXU_DIM))], axis=0)
    kcat = _mm(akv, wk_ext.reshape(kv_rank + LANES, H_C * MXU_DIM).astype(BF16), out_dtype=BF16)
    wv_t = jnp.concatenate([wkv[:, :, D_NOPE:].reshape(kv_rank, H_C * D_V_C).T,
                            jnp.zeros((H_C * D_V_C, LANES), F32)], axis=1)
    vt = _mm(wv_t.astype(BF16), akv.T, out_dtype=BF16)

    o_lat = _flash(qcat, kcat, vt, n_batch=n_batch, q_rows0=0, q_len=seq, segs=[(0, seq), (n_lat, ctx_len)])
    if not need_ctx_out:
        return o_lat
    o_ctx = _flash(qcat, kcat, vt, n_batch=n_batch, q_rows0=n_lat, q_len=ctx_len, segs=[(n_lat, ctx_len)])
    return jnp.concatenate([o_lat, o_ctx], axis=0)


def _modulations(cond, ada_down, ada_up, ada_bias, i):
    d = cond.shape[1]
    low = _mm(cond, ada_down, w_lead=(i,), out_dtype=BF16, prologue="silu")
    m = _mm(low, ada_up, w_lead=(i,), out_dtype=F32) + ada_bias[i][None, :]
    return m.reshape(cond.shape[0], N_MOD, d)


def kernel(x, c, ctx, c_ctx, ada_down, ada_up, ada_bias, ln_g, ln_b, ffn_w_in, ffn_w_out, gla_w_in, gla_w_gdown, gla_w_gup, gla_b_g, gla_norm_g, gla_w_out, mlstm_w_in, mlstm_w_if, mlstm_b_if, mlstm_norm_g, mlstm_w_out, mla_w_down, mla_q_norm_g, mla_w_uq, mla_kv_norm_g, mla_w_ukv, mla_w_out):
    n_batch, seq, d = x.shape
    ctx_len = ctx.shape[1]
    depth = ada_down.shape[0]
    n_lat = n_batch * seq
    n_ctx = n_batch * ctx_len
    t_all = n_lat + n_ctx
    dims = (n_batch, seq, ctx_len, d)
    alpha = (2.0 * depth) ** 0.25
    n_seg = n_batch + 1
    assert seq % ELEMENTWISE_TILE_ROWS == 0 and n_ctx % math.gcd(n_ctx, ELEMENTWISE_TILE_ROWS) == 0

    def seg_of_row(row):
        return jnp.minimum(row // seq, n_batch)

    xs = jnp.concatenate([x.reshape(n_lat, d), ctx.reshape(n_ctx, d)], axis=0)
    cond = jnp.concatenate([c, c_ctx[None, :], jnp.zeros((8 - n_seg, d), F32)], axis=0)

    ffn_w_out_b, gla_w_out_b = ffn_w_out.astype(BF16), gla_w_out.astype(BF16)
    mlstm_w_out_b, mla_w_out_b = mlstm_w_out.astype(BF16), mla_w_out.astype(BF16)

    def mod_vecs(m, s, weight):
        shift = m[:n_seg, 3 * s][:, None, :]
        scale1p = 1.0 + m[:n_seg, 3 * s + 1][:, None, :]
        gate_w = weight * m[:n_seg, 3 * s + 2][:, None, :]
        return shift, scale1p, gate_w

    mods = [_modulations(cond, ada_down, ada_up, ada_bias, i) for i in range(depth)]
    shift0, scale0, _ = mod_vecs(mods[0], 0, MACARON_WEIGHT)
    hcur = _modulate(xs, scale0, shift0, seg_of_row=seg_of_row)

    def close(y_in, w, w_lead, xs, gate_w, g, b, scale1p, shift, rows):
        return _proj_ln(y_in, w, xs, gate_w, g, b, scale1p, shift, rows=rows, seg_of_row=seg_of_row,
                        seg_rows=math.gcd(seq, n_ctx), alpha=alpha, w_lead=w_lead)

    for i in range(depth):
        last = i == depth - 1
        m = mods[i]
        _, _, gw0 = mod_vecs(m, 0, MACARON_WEIGHT)
        sh1, sc1, gw1 = mod_vecs(m, 1, 1.0)
        sh2, sc2, gw2 = mod_vecs(m, 2, MACARON_WEIGHT)

        act = _ffn1(hcur, ffn_w_in, t_all, w_lead=(i, 0))
        xs, hcur = close(act, ffn_w_out_b,(i, 0), xs, gw0, ln_g[i, 0], ln_b[i, 0], sc1, sh1, t_all)

        kind, j = i % 3, i // 3
        rows1 = n_lat if last else t_all
        if kind == 0:
            y_in = _gla_mixer(hcur, gla_w_in, j, gla_w_gdown[j], gla_w_gup[j], gla_b_g[j], gla_norm_g[j], dims)
            w_o = gla_w_out_b
        elif kind == 1:
            y_in = _mlstm_mixer(hcur, mlstm_w_in, j, mlstm_w_if[j], mlstm_b_if[j], mlstm_norm_g[j], dims)
            w_o = mlstm_w_out_b
        else:
            y_in = _mla_mixer(hcur, mla_w_down[j], mla_q_norm_g[j], mla_w_uq[j], mla_kv_norm_g[j], mla_w_ukv[j],
                              dims, not last)
            w_o = mla_w_out_b
        xs, hcur = close(y_in, w_o, (j,), xs, gw1, ln_g[i, 1], ln_b[i, 1], sc2, sh2, rows1)

        if last:
            sh_n, sc_n = sh2, sc2
        else:
            sh_n, sc_n, _ = mod_vecs(mods[i + 1], 0, MACARON_WEIGHT)
        act = _ffn1(hcur, ffn_w_in, rows1, w_lead=(i, 1))
        xs, hcur = close(act, ffn_w_out_b,(i, 1), xs, gw2, ln_g[i, 2], ln_b[i, 2], sc_n, sh_n, rows1)

    return xs.reshape(n_batch, seq, d)
```

```python
import functools
import math

import numpy as np
import jax
import jax.numpy as jnp
from jax import lax
from jax.experimental import pallas as pl
from jax.experimental.pallas import tpu as pltpu

F32 = jnp.float32
BF16 = jnp.bfloat16

N_MOD = 9
MACARON_WEIGHT = 0.5
LN_EPS = 1e-5
RMS_EPS = 1e-6
CHUNK = 64
H_A = 8
GATE_RANK_A = 16
GATE_NORMALIZER_A = 16.0
H_B = 8
GATE_SOFTCAP = 15.0
M_INIT = -1e30
H_C = 32
D_NOPE = 128
D_ROPE = 64
D_V_C = 128
ROPE_THETA = 10000.0
GRID_W = 64

LANES = 128
MXU_DIM = 256
VMEM_LIMIT_BYTES = 60 * 1024 * 1024
ROW_TILE_TARGET = 512
COL_TILE_TARGET = 2048
MM_VMEM_BUDGET = 55 * 1024 * 1024
ELEMENTWISE_TILE_ROWS = 256
LN_ROWS = 16
BF16_ROWS = 16
PROJ_K_TILE = 1280
Q_TILE = 1024
KV_TILE = 512
ONES_ROWS = 16

NEG_INF = float("-inf")
LOG2_E = math.log2(math.e)


def _cparams(*sem):
    return pltpu.CompilerParams(dimension_semantics=sem, vmem_limit_bytes=VMEM_LIMIT_BYTES)


def _pick_tile(n, target, mult=LANES):
    if n <= target:
        return n
    t = (target // mult) * mult
    while t > mult and n % t:
        t -= mult
    assert n % t == 0, (n, target, mult)
    return t


def _sigmoid(x):
    return 1.0 / (1.0 + jnp.exp(-x))


def _log_sigmoid(x):
    return jnp.minimum(x, 0.0) - jnp.log1p(jnp.exp(-jnp.abs(x)))


def _dot(a, b):
    return jnp.dot(a, b, preferred_element_type=F32)


def _dot_nt(a, b):
    return lax.dot_general(a, b, (((1,), (1,)), ((), ())), preferred_element_type=F32)


def _split2(x):
    hi = x.astype(BF16)
    lo = (x - hi.astype(F32)).astype(BF16)
    return hi, lo


def _wspec(lead, block, index_map):
    return pl.BlockSpec((None,) * len(lead) + block, lambda *g: tuple(lead) + index_map(*g))


def _resident_weights(w_ref, wb_ref):
    @pl.when(pl.program_id(1) == 0)
    def _():
        def body(r, carry):
            rows = pl.ds(pl.multiple_of(r * MXU_DIM, MXU_DIM), MXU_DIM)
            wb_ref[rows, :] = w_ref[rows, :].astype(wb_ref.dtype)
            return carry
        lax.fori_loop(0, w_ref.shape[0] // MXU_DIM, body, 0)
    return wb_ref


def _mm_kernel(*refs, prologue, epilogue, scale, cast_w):
    a_ref, w_ref = refs[0], refs[1]
    if cast_w:
        w_ref = _resident_weights(w_ref, refs[-1])
        refs = refs[:-1]
    o_ref = refs[-1]
    extra = refs[2:-1]
    a = a_ref[...]
    if prologue == "silu":
        af = a.astype(F32)
        a = (af * _sigmoid(af)).astype(BF16)
    acc = _dot(a, w_ref[...])
    if epilogue == "rope":
        cos = extra[0][...]
        sin = extra[1][...]
        lane = lax.broadcasted_iota(jnp.int32, cos.shape, 1)
        first_half = lane < (D_ROPE // 2)
        for g in range(acc.shape[1] // MXU_DIM):
            lo = g * MXU_DIM
            o_ref[:, lo:lo + LANES] = (acc[:, lo:lo + LANES] * scale).astype(o_ref.dtype)
            x = acc[:, lo + LANES:lo + 2 * LANES]
            swapped = jnp.where(first_half, pltpu.roll(x, LANES - D_ROPE // 2, 1), pltpu.roll(x, D_ROPE // 2, 1))
            o_ref[:, lo + LANES:lo + 2 * LANES] = ((x * cos + swapped * sin) * scale).astype(o_ref.dtype)
    else:
        o_ref[...] = acc.astype(o_ref.dtype)


def _mm_tiles(m, kk, n, a_bytes, w_bytes, out_bytes, cast_w, tn_mult):
    tm = _pick_tile(m, ROW_TILE_TARGET, 8)
    tn_target = COL_TILE_TARGET if m > 8 else n
    while True:
        tn = _pick_tile(n, tn_target, tn_mult)
        need = 2 * tm * kk * a_bytes + 2 * kk * tn * w_bytes + 2 * tm * tn * out_bytes + tm * tn * 4
        need += kk * tn * 2 if cast_w else 0
        if need <= MM_VMEM_BUDGET or tn <= tn_mult:
            return tm, tn
        tn_target = tn - tn_mult


def _mm(a, w, *, out_dtype, w_lead=(), rows=None, prologue=None, epilogue=None, cos=None, sin=None, scale=1.0):
    m = a.shape[0] if rows is None else rows
    kk, n = w.shape[-2:]
    assert a.shape[1] == kk or kk % LANES == 0
    cast_w = w.dtype != BF16
    tm, tn = _mm_tiles(m, kk, n, a.dtype.itemsize, w.dtype.itemsize, jnp.dtype(out_dtype).itemsize, cast_w,
                       MXU_DIM if epilogue == "rope" else LANES)
    in_specs = [pl.BlockSpec((tm, kk), lambda j, i: (i, 0)),
                _wspec(w_lead, (kk, tn), lambda j, i: (0, j))]
    args = [a, w]
    if epilogue == "rope":
        in_specs += [pl.BlockSpec((tm, LANES), lambda j, i: (i, 0))] * 2
        args += [cos, sin]
    return pl.pallas_call(
        functools.partial(_mm_kernel, prologue=prologue, epilogue=epilogue, scale=scale, cast_w=cast_w),
        out_shape=jax.ShapeDtypeStruct((m, n), out_dtype),
        grid=(n // tn, m // tm),
        in_specs=in_specs,
        out_specs=pl.BlockSpec((tm, tn), lambda j, i: (i, j)),
        scratch_shapes=[pltpu.VMEM((kk, tn), BF16)] if cast_w else [],
        compiler_params=_cparams("parallel", "arbitrary"),
        name="mm_" + (prologue or "p") + "_" + (epilogue or "e"),
    )(*args)


def _ffn1_kernel(h_ref, wa_ref, wu_ref, o_ref, *scratch):
    if scratch:
        wa_ref = _resident_weights(wa_ref, scratch[0])
        wu_ref = _resident_weights(wu_ref, scratch[1])
    h = h_ref[...]
    a = _dot(h, wa_ref[...])
    u = _dot(h, wu_ref[...])
    o_ref[...] = (a * _sigmoid(a) * u).astype(o_ref.dtype)


def _ffn1(h, w_in, rows, w_lead=()):
    d, f2 = w_in.shape[-2:]
    f = f2 // 2
    cast_w = w_in.dtype != BF16
    tm = _pick_tile(rows, ROW_TILE_TARGET, 8)
    tn = _pick_tile(f, 512)
    nj = f // tn
    return pl.pallas_call(
        _ffn1_kernel,
        out_shape=jax.ShapeDtypeStruct((rows, f), BF16),
        grid=(nj, rows // tm),
        in_specs=[pl.BlockSpec((tm, d), lambda j, i: (i, 0)),
                  _wspec(w_lead, (d, tn), lambda j, i: (0, j)),
                  _wspec(w_lead, (d, tn), lambda j, i: (0, j + nj))],
        out_specs=pl.BlockSpec((tm, tn), lambda j, i: (i, j)),
        scratch_shapes=[pltpu.VMEM((d, tn), BF16)] * 2 if cast_w else [],
        compiler_params=_cparams("parallel", "arbitrary"),
        name="ffn1",
    )(h, w_in, w_in)


def _proj_ln_kernel(act_ref, w_ref, x_ref, gw_ref, lng_ref, lnb_ref, sc_ref, sh_ref, xo_ref, ho_ref,
                    acc_even_ref, acc_odd_ref, stage_ref, *, alpha, n_tiles):
    i = pl.program_id(0)
    k = pl.program_id(1)
    slab = xo_ref.shape[0]
    accs = (acc_even_ref, acc_odd_ref)

    def accumulate(cur_ref):
        cur_ref[...] = _dot(act_ref[...], w_ref[...]) + cur_ref[...]

    def stage_slab(prev_ref):
        rows = pl.ds(pl.multiple_of(k * slab, slab), slab)
        stage_ref[...] = prev_ref[rows, :]
        prev_ref[rows, :] = jnp.zeros((slab, prev_ref.shape[1]), F32)

    def normalise_piece(r):
        lo = r * LN_ROWS
        v = alpha * x_ref[lo:lo + LN_ROWS, :] + gw_ref[...] * stage_ref[lo:lo + LN_ROWS, :]
        mu = jnp.mean(v, axis=-1, keepdims=True)
        vc = v - mu
        var = jnp.mean(vc * vc, axis=-1, keepdims=True)
        xn = vc * lax.rsqrt(var + LN_EPS) * lng_ref[...] + lnb_ref[...]
        xo_ref[lo:lo + LN_ROWS, :] = xn
        ho_ref[lo:lo + LN_ROWS, :] = (xn * sc_ref[...] + sh_ref[...]).astype(ho_ref.dtype)
        return xn

    def normalise():
        for r in range(slab // LN_ROWS):
            normalise_piece(r)

    def accumulate_beside_normalise(cur_ref):
        n_pieces = slab // LN_ROWS
        n_groups = math.gcd(n_pieces, w_ref.shape[1] // MXU_DIM)
        gcols = w_ref.shape[1] // n_groups
        per_group = n_pieces // n_groups
        anchor = None
        for c in range(n_groups):
            cols = slice(c * gcols, (c + 1) * gcols)
            w = w_ref[:, cols]
            if anchor is not None:
                head = jnp.concatenate([w[:BF16_ROWS, :LANES] + anchor, w[:BF16_ROWS, LANES:]], axis=1)
                w = jnp.concatenate([head, w[BF16_ROWS:, :]], axis=0)
            cur_ref[:, cols] = _dot(act_ref[...], w) + cur_ref[:, cols]
            for r in range(c * per_group, (c + 1) * per_group):
                xn = normalise_piece(r)
            anchor = jnp.minimum(jnp.abs(xn[:BF16_ROWS, :LANES]), 0.0).astype(w_ref.dtype)

    @pl.when(jnp.logical_and(i == 0, k == 0))
    def _():
        acc_even_ref[...] = jnp.zeros(acc_even_ref.shape, F32)
        acc_odd_ref[...] = jnp.zeros(acc_odd_ref.shape, F32)

    @pl.when(i == 0)
    def _():
        accumulate(acc_even_ref)

    for parity in range(2):
        @pl.when(jnp.logical_and(jnp.logical_and(i > 0, i < n_tiles), lax.rem(i, 2) == parity))
        def _(parity=parity):
            stage_slab(accs[1 - parity])
            accumulate_beside_normalise(accs[parity])

    @pl.when(i == n_tiles)
    def _():
        stage_slab(accs[(n_tiles - 1) % 2])
        normalise()


def _proj_ln(act, w, x, gate_w, ln_g, ln_b, scale1p, shift, *, rows, seg_of_row, seg_rows, alpha, w_lead=()):
    kk, d = w.shape[-2:]
    tm = math.gcd(math.gcd(rows, ROW_TILE_TARGET), seg_rows)
    n_tiles = rows // tm
    tk = kk
    for cand in range((min(PROJ_K_TILE, kk) // MXU_DIM) * MXU_DIM, 0, -MXU_DIM):
        if kk % cand == 0 and tm % ((kk // cand) * LN_ROWS) == 0:
            tk = cand
            break
    n_k = kk // tk
    slab_rows = tm // n_k
    assert tm % LN_ROWS == 0

    def slab_index(i, k):
        return jnp.where(i == 0, 0, (i - 1) * n_k + k)

    vec = pl.BlockSpec((None, 1, d), lambda i, k: (seg_of_row(slab_index(i, k) * slab_rows), 0, 0))
    cvec = pl.BlockSpec((1, d), lambda i, k: (0, 0))
    slab = pl.BlockSpec((slab_rows, d), lambda i, k: (slab_index(i, k), 0))
    return pl.pallas_call(
        functools.partial(_proj_ln_kernel, alpha=alpha, n_tiles=n_tiles),
        out_shape=(jax.ShapeDtypeStruct((rows, d), F32), jax.ShapeDtypeStruct((rows, d), BF16)),
        grid=(n_tiles + 1, n_k),
        in_specs=[pl.BlockSpec((tm, tk), lambda i, k: (jnp.minimum(i, n_tiles - 1), k)),
                  _wspec(w_lead, (tk, d), lambda i, k: (k, 0)),
                  slab, vec, cvec, cvec, vec, vec],
        out_specs=(slab, slab),
        scratch_shapes=[pltpu.VMEM((tm, d), F32), pltpu.VMEM((tm, d), F32), pltpu.VMEM((slab_rows, d), F32)],
        compiler_params=_cparams("arbitrary", "arbitrary"),
        name="proj_ln",
    )(act, w, x, gate_w, ln_g.reshape(1, d), ln_b.reshape(1, d), scale1p, shift)


def _modulate_kernel(x_ref, sc_ref, sh_ref, o_ref):
    o_ref[...] = (x_ref[...] * sc_ref[...] + sh_ref[...]).astype(o_ref.dtype)


def _modulate(x, scale1p, shift, *, seg_of_row):
    t, d = x.shape
    tr = math.gcd(t, ELEMENTWISE_TILE_ROWS)
    vec = pl.BlockSpec((None, 1, d), lambda i: (seg_of_row(i * tr), 0, 0))
    return pl.pallas_call(
        _modulate_kernel,
        out_shape=jax.ShapeDtypeStruct((t, d), BF16),
        grid=(t // tr,),
        in_specs=[pl.BlockSpec((tr, d), lambda i: (i, 0)), vec, vec],
        out_specs=pl.BlockSpec((tr, d), lambda i: (i, 0)),
        compiler_params=_cparams("parallel"),
        name="modulate",
    )(x, scale1p, shift)


def _chunk_row_block(b, dr, c, *, n_ctx_chunks, n_lat_chunks, n_batch):
    jc = jnp.where(dr == 1, n_ctx_chunks - 1 - c, c)
    jl = jnp.where(dr == 1, n_lat_chunks - 1 - (c - n_ctx_chunks), c - n_ctx_chunks)
    ctx_blk = n_batch * n_lat_chunks + b * n_ctx_chunks + jc
    lat_blk = b * n_lat_chunks + jl
    return jnp.where(c < n_ctx_chunks, ctx_blk, lat_blk)


_GLA_LEVELS = (32, 16, 8, 4, 2, 1)
_GLA_ROWS_B = len(_GLA_LEVELS) * CHUNK
_GLA_ROWS_BL = _GLA_ROWS_B + CHUNK
_GLA_ROWS_ONE = _GLA_ROWS_BL + CHUNK
_GLA_A_ROWS = _GLA_ROWS_ONE + CHUNK


def _gla_constants():
    c = CHUNK
    a = np.zeros((_GLA_A_ROWS, c), np.float32)
    masks = np.zeros((len(_GLA_LEVELS) + 1, c, c), np.float32)
    t = np.arange(c)
    for l, s in enumerate(_GLA_LEVELS):
        for i in range(c):
            r = (i // (2 * s)) * 2 * s + s - 1
            if i % (2 * s) >= s:
                a[l * c + i] = (t > r) & (t <= i)
                lo = (i // (2 * s)) * 2 * s
                masks[l, i, lo:lo + s] = 1.0
            else:
                a[l * c + i] = (t > i) & (t <= r)
    for i in range(c):
        a[_GLA_ROWS_B + i] = t <= i
        a[_GLA_ROWS_BL + i] = t > i
        a[_GLA_ROWS_ONE + i] = 1.0
    masks[len(_GLA_LEVELS)] = np.eye(c)
    a_rev = a.reshape(-1, c, c)[:, ::-1, ::-1].reshape(-1, c)
    masks_rev = masks[:, ::-1, ::-1]
    return (np.stack([a, a_rev]).astype(np.float32), np.stack([masks, masks_rev]).astype(np.float32))


def _head_readout(o, gate, norm_gain, kind):
    g = gate.astype(F32)
    act = g * _sigmoid(g) if kind == "silu" else _sigmoid(g)
    r = lax.rsqrt(jnp.mean(o * o, axis=-1, keepdims=True) + RMS_EPS)
    return (o * r * norm_gain * act).astype(BF16)


def _gla_kernel(*refs, direction, readout, q_scale, n_heads, dk, dv):
    q_ref, k_ref, v_ref, g_ref, wup_ref, bg_ref, a_ref, mask_ref = refs[:8]
    o_ref, s_ref = refs[-2:]
    if readout:
        other_ref, gate_ref, ng_ref = refs[8:11]
    dr = direction
    c = pl.program_id(1)

    @pl.when(c == 0)
    def _():
        s_ref[...] = jnp.zeros(s_ref.shape, F32)

    pre = _dot(g_ref[...], wup_ref[...]) + bg_ref[...]
    la_all = _log_sigmoid(pre) * (1.0 / GATE_NORMALIZER_A)
    amat = a_ref[...]
    row = lax.broadcasted_iota(jnp.int32, (CHUNK, 1), 0)
    if dr == 1:
        row = CHUNK - 1 - row
    reps = dv // LANES

    def exponents(h):
        hi, lo = _split2(la_all[:, h * dk:(h + 1) * dk])
        return jnp.exp(_dot(amat, hi) + _dot(amat, lo))

    def intra(h, e):
        ks = slice(h * dk, (h + 1) * dk)
        qf = q_ref[:, ks].astype(F32) * q_scale
        kf = k_ref[:, ks].astype(F32)
        p = mask_ref[len(_GLA_LEVELS)] * _dot_nt(qf.astype(BF16), kf.astype(BF16))
        for l, s in enumerate(_GLA_LEVELS):
            upper = (row & s) != 0
            m = (jnp.where(upper, qf, kf) * e[l * CHUNK:(l + 1) * CHUNK]).astype(BF16)
            p = p + mask_ref[l] * _dot_nt(m, m)
        q_hat = (qf * e[_GLA_ROWS_B:_GLA_ROWS_B + CHUNK]).astype(BF16)
        k_hat = kf * e[_GLA_ROWS_BL:_GLA_ROWS_BL + CHUNK]
        ke_t = jnp.concatenate([k_hat, e[_GLA_ROWS_ONE:_GLA_ROWS_ONE + CHUNK]], axis=0).T
        return p.astype(BF16), q_hat, ke_t

    def output_and_state(h, p, q_hat, ke_t):
        vs = slice(h * dv, (h + 1) * dv)
        v = v_ref[:, vs]
        s_old = s_ref[h]
        o = _dot(p, v) + _dot(q_hat, s_old.astype(BF16))
        if readout:
            o_ref[:, vs] = _head_readout(o + other_ref[:, vs], gate_ref[:, vs], ng_ref[:, vs], "silu")
        else:
            o_ref[:, vs] = o
        decay = jnp.concatenate([ke_t[:, CHUNK:], ke_t[:, CHUNK:]], axis=1)
        v_pad = jnp.concatenate([v, jnp.zeros_like(v)], axis=0)
        s_ref[h] = s_old * jnp.concatenate([decay] * reps, axis=1) + _dot(ke_t.astype(BF16), v_pad)

    e_of, intra_of = {}, {}
    for step in range(n_heads + 2):
        if step < n_heads:
            e_of[step] = exponents(step)
        if 0 <= step - 1 < n_heads:
            intra_of[step - 1] = intra(step - 1, e_of.pop(step - 1))
        if 0 <= step - 2 < n_heads:
            output_and_state(step - 2, *intra_of.pop(step - 2))


def _gla_scan(z, g_low, wup_ext, b_g, norm_g, *, n_batch, seq, ctx_len, dk, dv):
    t = z.shape[0]
    qk, d = H_A * dk, H_A * dv
    assert 2 * qk == d
    ncc, nlc = ctx_len // CHUNK, seq // CHUNK
    a_np, mask_np = _gla_constants()
    a_c = jnp.asarray(a_np, BF16)
    mask_c = jnp.asarray(mask_np, F32)

    def one_direction(dr, other):
        def rb(b, c):
            return _chunk_row_block(b, dr, c, n_ctx_chunks=ncc, n_lat_chunks=nlc, n_batch=n_batch)

        readout = other is not None
        in_specs = [pl.BlockSpec((CHUNK, qk), lambda b, c: (rb(b, c), 0)),
                    pl.BlockSpec((CHUNK, qk), lambda b, c: (rb(b, c), 1)),
                    pl.BlockSpec((CHUNK, d), lambda b, c: (rb(b, c), 1)),
                    pl.BlockSpec((CHUNK, LANES), lambda b, c: (rb(b, c), 0)),
                    pl.BlockSpec((None, LANES, qk), lambda b, c: (dr, 0, 0)),
                    pl.BlockSpec((None, 1, qk), lambda b, c: (dr, 0, 0)),
                    pl.BlockSpec((None, _GLA_A_ROWS, CHUNK), lambda b, c: (dr, 0, 0)),
                    pl.BlockSpec((None, len(_GLA_LEVELS) + 1, CHUNK, CHUNK), lambda b, c: (dr, 0, 0, 0))]
        args = [z, z, z, g_low, wup_ext, b_g, a_c, mask_c]
        if readout:
            in_specs += [pl.BlockSpec((CHUNK, d), lambda b, c: (rb(b, c), 0)),
                         pl.BlockSpec((CHUNK, d), lambda b, c: (rb(b, c), 2)),
                         pl.BlockSpec((1, d), lambda b, c: (0, 0))]
            args += [other, z, norm_g.reshape(1, d)]
        return pl.pallas_call(
            functools.partial(_gla_kernel, direction=dr, readout=readout, q_scale=dk ** -0.5,
                              n_heads=H_A, dk=dk, dv=dv),
            out_shape=jax.ShapeDtypeStruct((t, d), BF16 if readout else F32),
            grid=(n_batch, ncc + nlc),
            in_specs=in_specs,
            out_specs=pl.BlockSpec((CHUNK, d), lambda b, c: (rb(b, c), 0)),
            scratch_shapes=[pltpu.VMEM((H_A, dk, dv), F32)],
            compiler_params=_cparams("arbitrary", "arbitrary"),
            name="gla_scan",
        )(*args)

    return one_direction(0, one_direction(1, None))


def _mlstm_kernel(*refs, direction, readout, q_scale, n_heads, dk, dv):
    q_ref, k_ref, v_ref, gates_ref = refs[:4]
    o_ref, cn_ref, m_ref = refs[-3:]
    if readout:
        other_ref, gate_ref, ng_ref = refs[4:7]
    dr = direction
    c = pl.program_id(1)

    @pl.when(c == 0)
    def _():
        cn_ref[...] = jnp.zeros(cn_ref.shape, F32)
        m_ref[...] = jnp.full(m_ref.shape, M_INIT, F32)

    ii = lax.broadcasted_iota(jnp.int32, (CHUNK, CHUNK), 0)
    jj = lax.broadcasted_iota(jnp.int32, (CHUNK, CHUNK), 1)
    eye = ii == jj
    causal = (jj - ii) * (1 - 2 * dr) <= 0
    lane = lax.broadcasted_iota(jnp.int32, (CHUNK, LANES), 1)
    ones_col = jnp.where(lane == 0, 1.0, 0.0).astype(BF16)

    def to_col(r):
        return jnp.sum(jnp.where(eye, jnp.broadcast_to(r, (CHUNK, CHUNK)), 0.0), axis=1, keepdims=True)

    def to_row(cl):
        return jnp.sum(jnp.where(eye, jnp.broadcast_to(cl, (CHUNK, CHUNK)), 0.0), axis=0, keepdims=True)

    def gate_weights(h):
        ic = GATE_SOFTCAP * jnp.tanh(gates_ref[h] * (1.0 / GATE_SOFTCAP))
        fc = _log_sigmoid(GATE_SOFTCAP * jnp.tanh(gates_ref[n_heads + h] * (1.0 / GATE_SOFTCAP)))
        b_col = jnp.sum(jnp.where(causal, jnp.broadcast_to(fc, (CHUNK, CHUNK)), 0.0), axis=1, keepdims=True)
        b_row = to_row(b_col)
        m_old = m_ref[h][0:1, 0:1]
        d_log = jnp.where(causal, b_col - b_row + ic, NEG_INF)
        inter_log = b_col + m_old
        m_row = jnp.maximum(inter_log, jnp.max(d_log, axis=1, keepdims=True))
        w_intra = jnp.exp(d_log - m_row)
        w_inter = jnp.exp(inter_log - m_row)
        floor = jnp.exp(-m_row)
        b_last = jnp.sum(fc, axis=1, keepdims=True)
        w_log = b_last - b_row + ic
        m_new = jnp.maximum(b_last + m_old, jnp.max(w_log, axis=1, keepdims=True))
        keep = jnp.exp(b_last + m_old - m_new)
        w_col = to_col(jnp.exp(w_log - m_new))
        m_ref[h] = jnp.broadcast_to(m_new, m_ref.shape[1:])
        return w_intra, w_inter, floor, keep, w_col

    def output_and_state(h, w_intra, w_inter, floor, keep, w_col):
        ks = slice(h * dk, (h + 1) * dk)
        vs = slice(h * dv, (h + 1) * dv)
        q = (q_ref[:, ks].astype(F32) * q_scale).astype(BF16)
        k = k_ref[:, ks]
        v_ext = jnp.concatenate([v_ref[:, vs], ones_col], axis=1)
        s = _dot_nt(q, k) * w_intra
        cn_old = cn_ref[h]
        nd = _dot(s.astype(BF16), v_ext) + w_inter * _dot(q, cn_old.astype(BF16))
        den = nd[:, dv:dv + 1]
        o = nd[:, :dv] / jnp.maximum(jnp.abs(den), floor)
        if readout:
            o_ref[:, vs] = _head_readout(o + other_ref[:, vs], gate_ref[:, vs], ng_ref[:, vs], "sigmoid")
        else:
            o_ref[:, vs] = o
        wk = k.astype(F32) * w_col
        wk_t = jnp.concatenate([wk, jnp.zeros_like(wk)], axis=0).T
        v_pad = jnp.concatenate([v_ext, jnp.zeros_like(v_ext)], axis=0)
        cn_ref[h] = keep * cn_old + _dot(wk_t.astype(BF16), v_pad)

    weights = {}
    for step in range(n_heads + 1):
        if step < n_heads:
            weights[step] = gate_weights(step)
        if step >= 1:
            output_and_state(step - 1, *weights.pop(step - 1))


def _mlstm_scan(z, gates, norm_g, *, n_batch, seq, ctx_len, dk, dv):
    t = z.shape[0]
    qk, d = H_B * dk, H_B * dv
    assert 2 * qk == d
    ncc, nlc = ctx_len // CHUNK, seq // CHUNK

    def one_direction(dr, other):
        def rb(b, c):
            return _chunk_row_block(b, dr, c, n_ctx_chunks=ncc, n_lat_chunks=nlc, n_batch=n_batch)

        readout = other is not None
        in_specs = [pl.BlockSpec((CHUNK, qk), lambda b, c: (rb(b, c), 0)),
                    pl.BlockSpec((CHUNK, qk), lambda b, c: (rb(b, c), 1)),
                    pl.BlockSpec((CHUNK, d), lambda b, c: (rb(b, c), 1)),
                    pl.BlockSpec((None, None, 2 * H_B, 1, CHUNK), lambda b, c: (dr, rb(b, c), 0, 0, 0))]
        args = [z, z, z, gates]
        if readout:
            in_specs += [pl.BlockSpec((CHUNK, d), lambda b, c: (rb(b, c), 0)),
                         pl.BlockSpec((CHUNK, d), lambda b, c: (rb(b, c), 2)),
                         pl.BlockSpec((1, d), lambda b, c: (0, 0))]
            args += [other, z, norm_g.reshape(1, d)]
        return pl.pallas_call(
            functools.partial(_mlstm_kernel, direction=dr, readout=readout, q_scale=dk ** -0.5,
                              n_heads=H_B, dk=dk, dv=dv),
            out_shape=jax.ShapeDtypeStruct((t, d), BF16 if readout else F32),
            grid=(n_batch, ncc + nlc),
            in_specs=in_specs,
            out_specs=pl.BlockSpec((CHUNK, d), lambda b, c: (rb(b, c), 0)),
            scratch_shapes=[pltpu.VMEM((H_B, dk, dv + LANES), F32), pltpu.VMEM((H_B, 8, LANES), F32)],
            compiler_params=_cparams("arbitrary", "arbitrary"),
            name="mlstm_scan",
        )(*args)

    return one_direction(0, one_direction(1, None))


def _flash_kernel(*refs, n_seg):
    q_ref = refs[0]
    kv = refs[1:1 + 2 * n_seg]
    o_ref = refs[1 + 2 * n_seg]
    vx_refs = refs[2 + 2 * n_seg:]
    q = q_ref[...]
    tq = q.shape[0]
    m = jnp.full((1, tq), NEG_INF, F32)
    acc = jnp.zeros((D_V_C + ONES_ROWS, tq), F32)

    @pl.when(pl.program_id(2) == 0)
    def _():
        for sg in range(n_seg):
            vx_refs[sg][:D_V_C, :] = kv[2 * sg + 1][...]
            vx_refs[sg][D_V_C:, :] = jnp.ones((ONES_ROWS, vx_refs[sg].shape[1]), BF16)

    blocks = []
    for sg in range(n_seg):
        k_ref, vt_ref = kv[2 * sg], vx_refs[sg]
        length = k_ref.shape[0]
        tk = min(KV_TILE, length)
        blocks += [(k_ref, vt_ref, t * tk, tk) for t in range(length // tk)]

    def scores(blk):
        k_ref, _, lo, tk = blk
        return _dot_nt(k_ref[lo:lo + tk, :], q)

    st_next = scores(blocks[0])
    pending = None
    for t, (_, vt_ref, lo, tk) in enumerate(blocks):
        st = st_next
        if t + 1 < len(blocks):
            st_next = scores(blocks[t + 1])
        m_new = jnp.maximum(m, jnp.max(st, axis=0, keepdims=True))
        alpha = jnp.exp2(m - m_new)
        p = jnp.exp2(st - m_new).astype(BF16)
        m = m_new
        if pending is not None:
            a_prev, vt_prev, p_prev = pending
            acc = a_prev * acc + _dot(vt_prev, p_prev)
        pending = (alpha, vt_ref[:, lo:lo + tk], p)
    a_prev, vt_prev, p_prev = pending
    acc = a_prev * acc + _dot(vt_prev, p_prev)
    o_ref[...] = (acc[:D_V_C] / acc[D_V_C:D_V_C + 1]).T.astype(o_ref.dtype)


def _flash(qcat, kcat, vt, *, n_batch, q_rows0, q_len, segs):
    tq = _pick_tile(q_len, Q_TILE, 8)
    nq = q_len // tq
    in_specs = [pl.BlockSpec((tq, MXU_DIM), lambda b, h, i: ((q_rows0 + b * q_len) // tq + i, h))]
    args = [qcat]
    for row0, length in segs:
        in_specs.append(pl.BlockSpec((length, MXU_DIM), lambda b, h, i, row0=row0, length=length: (row0 // length + b, h)))
        in_specs.append(pl.BlockSpec((D_V_C, length), lambda b, h, i, row0=row0, length=length: (h, row0 // length + b)))
        args += [kcat, vt]
    return pl.pallas_call(
        functools.partial(_flash_kernel, n_seg=len(segs)),
        out_shape=jax.ShapeDtypeStruct((n_batch * q_len, H_C * D_V_C), BF16),
        grid=(n_batch, H_C, nq),
        in_specs=in_specs,
        out_specs=pl.BlockSpec((tq, D_V_C), lambda b, h, i: (b * nq + i, h)),
        scratch_shapes=[pltpu.VMEM((D_V_C + ONES_ROWS, length), BF16) for _, length in segs],
        compiler_params=_cparams("parallel", "parallel", "arbitrary"),
        name="flash",
    )(*args)


def _mla_prep_kernel(z_ref, gq_ref, gkv_ref, cos_ref, sin_ref, cq_ref, akv_ref, *, q_rank, kv_rank):
    cq = z_ref[:, :q_rank]
    cq_ref[...] = (cq * lax.rsqrt(jnp.mean(cq * cq, axis=-1, keepdims=True) + RMS_EPS) * gq_ref[...]).astype(BF16)
    ckv = z_ref[:, q_rank:q_rank + kv_rank]
    akv_ref[:, :kv_rank] = (ckv * lax.rsqrt(jnp.mean(ckv * ckv, axis=-1, keepdims=True) + RMS_EPS)
                            * gkv_ref[...]).astype(BF16)
    x = z_ref[:, q_rank + kv_rank:q_rank + kv_rank + LANES]
    lane = lax.broadcasted_iota(jnp.int32, x.shape, 1)
    swapped = jnp.where(lane < D_ROPE // 2, pltpu.roll(x, LANES - D_ROPE // 2, 1), pltpu.roll(x, D_ROPE // 2, 1))
    akv_ref[:, kv_rank:] = (x * cos_ref[...] + swapped * sin_ref[...]).astype(BF16)


def _mla_prep(zd, gq, gkv, cos, sin, *, q_rank, kv_rank):
    t = zd.shape[0]
    tr = _pick_tile(t, 256, 8)
    return pl.pallas_call(
        functools.partial(_mla_prep_kernel, q_rank=q_rank, kv_rank=kv_rank),
        out_shape=(jax.ShapeDtypeStruct((t, q_rank), BF16), jax.ShapeDtypeStruct((t, kv_rank + LANES), BF16)),
        grid=(t // tr,),
        in_specs=[pl.BlockSpec((tr, zd.shape[1]), lambda i: (i, 0)),
                  pl.BlockSpec((1, q_rank), lambda i: (0, 0)),
                  pl.BlockSpec((1, kv_rank), lambda i: (0, 0)),
                  pl.BlockSpec((tr, LANES), lambda i: (i, 0)),
                  pl.BlockSpec((tr, LANES), lambda i: (i, 0))],
        out_specs=(pl.BlockSpec((tr, q_rank), lambda i: (i, 0)),
                   pl.BlockSpec((tr, kv_rank + LANES), lambda i: (i, 0))),
        compiler_params=_cparams("parallel"),
        name="mla_prep",
    )(zd, gq.reshape(1, q_rank), gkv.reshape(1, kv_rank), cos, sin)


def _rope_tables(n_batch, seq, ctx_len):
    n_rows = seq // GRID_W
    rows = np.repeat(np.arange(n_rows, dtype=np.float32), GRID_W)
    cols = np.tile(np.arange(GRID_W, dtype=np.float32), n_rows)
    half = D_ROPE // 2
    inv_freq = ROPE_THETA ** (-jnp.arange(0, half, 2, dtype=F32) / half)
    ang = jnp.concatenate([jnp.asarray(rows)[:, None] * inv_freq, jnp.asarray(cols)[:, None] * inv_freq], axis=-1)
    cos, sin = jnp.cos(ang), jnp.sin(ang)
    pad1 = jnp.ones((seq, LANES - D_ROPE), F32)
    pad0 = jnp.zeros((seq, LANES - D_ROPE), F32)
    cos_l = jnp.concatenate([cos, cos, pad1], axis=1)
    sin_l = jnp.concatenate([-sin, sin, pad0], axis=1)
    cos_t = jnp.concatenate([jnp.tile(cos_l, (n_batch, 1)), jnp.ones((n_batch * ctx_len, LANES), F32)], axis=0)
    sin_t = jnp.concatenate([jnp.tile(sin_l, (n_batch, 1)), jnp.zeros((n_batch * ctx_len, LANES), F32)], axis=0)
    return cos_t, sin_t


def _gla_mixer(h, w_in_b, j, w_gdown, w_gup, b_g, norm_g, dims):
    n_batch, seq, ctx_len, d = dims
    dk, dv = d // (2 * H_A), d // H_A
    qk = H_A * dk
    z = _mm(h, w_in_b, w_lead=(j,), out_dtype=BF16)
    w_low = jnp.concatenate([w_gdown[0], w_gdown[1],
                             jnp.zeros((d, LANES - 2 * GATE_RANK_A), F32)], axis=1).astype(BF16)
    g_low = _mm(h, w_low, out_dtype=BF16)
    wup = jnp.zeros((2, LANES, qk), F32)
    wup = wup.at[0, :GATE_RANK_A].set(w_gup[0]).at[1, GATE_RANK_A:2 * GATE_RANK_A].set(w_gup[1]).astype(BF16)
    return _gla_scan(z, g_low, wup, b_g.reshape(2, 1, qk), norm_g,
                     n_batch=n_batch, seq=seq, ctx_len=ctx_len, dk=dk, dv=dv)


def _mlstm_mixer(h, w_in_b, j, w_if, b_if, norm_g, dims):
    n_batch, seq, ctx_len, d = dims
    dk, dv = d // (2 * H_B), d // H_B
    qk = H_B * dk
    t = h.shape[0]
    z = _mm(h, w_in_b, w_lead=(j,), out_dtype=BF16)
    w_g = jnp.concatenate([w_if[0], w_if[1], jnp.zeros((d, LANES - 4 * H_B), F32)], axis=1).astype(BF16)
    pre = _mm(h, w_g, out_dtype=F32)[:, :4 * H_B] + jnp.concatenate([b_if[0], b_if[1]])[None, :]
    gates = pre.reshape(t // CHUNK, CHUNK, 2, 2 * H_B).transpose(2, 0, 3, 1)[:, :, :, None, :]
    return _mlstm_scan(z, gates, norm_g, n_batch=n_batch, seq=seq, ctx_len=ctx_len, dk=dk, dv=dv)


def _mla_mixer(h, w_down, q_norm_g, w_uq, kv_norm_g, w_ukv, dims, need_ctx_out):
    n_batch, seq, ctx_len, d = dims
    q_rank, kv_rank = d // 4, d // 8
    n_lat = n_batch * seq
    scale = (D_NOPE + D_ROPE) ** -0.5 * LOG2_E
    cos_t, sin_t = _rope_tables(n_batch, seq, ctx_len)

    n_down = q_rank + kv_rank + LANES
    w_down_ext = jnp.concatenate([w_down, jnp.zeros((d, n_down - w_down.shape[1]), F32)], axis=1).astype(BF16)
    zd = _mm(h, w_down_ext, out_dtype=F32)
    cqn, akv = _mla_prep(zd, q_norm_g, kv_norm_g, cos_t, sin_t, q_rank=q_rank, kv_rank=kv_rank)

    wq = w_uq.reshape(q_rank, H_C, D_NOPE + D_ROPE)
    wq = jnp.concatenate([wq, jnp.zeros((q_rank, H_C, MXU_DIM - D_NOPE - D_ROPE), F32)], axis=2)
    qcat = _mm(cqn, wq.reshape(q_rank, H_C * MXU_DIM).astype(BF16), out_dtype=BF16,
               epilogue="rope", cos=cos_t, sin=sin_t, scale=scale)

    wkv = w_ukv.reshape(kv_rank, H_C, D_NOPE + D_V_C)
    wk_top = jnp.concatenate([wkv[:, :, :D_NOPE], jnp.zeros((kv_rank, H_C, MXU_DIM - D_NOPE), F32)], axis=2)
    ident = jnp.concatenate([jnp.zeros((LANES, D_NOPE), F32), jnp.eye(LANES, dtype=F32)], axis=1)
    wk_ext = jnp.concatenate([wk_top, jnp.broadcast_to(ident[:, None, :], (LANES, H_C, MXU_DIM))], axis=0)
    kcat = _mm(akv, wk_ext.reshape(kv_rank + LANES, H_C * MXU_DIM).astype(BF16), out_dtype=BF16)
    wv_t = jnp.concatenate([wkv[:, :, D_NOPE:].reshape(kv_rank, H_C * D_V_C).T,
                            jnp.zeros((H_C * D_V_C, LANES), F32)], axis=1)
    vt = _mm(wv_t.astype(BF16), akv.T, out_dtype=BF16)

    o_lat = _flash(qcat, kcat, vt, n_batch=n_batch, q_rows0=0, q_len=seq, segs=[(0, seq), (n_lat, ctx_len)])
    if not need_ctx_out:
        return o_lat
    o_ctx = _flash(qcat, kcat, vt, n_batch=n_batch, q_rows0=n_lat, q_len=ctx_len, segs=[(n_lat, ctx_len)])
    return jnp.concatenate([o_lat, o_ctx], axis=0)


def _modulations(cond, ada_down, ada_up, ada_bias, i):
    d = cond.shape[1]
    low = _mm(cond, ada_down, w_lead=(i,), out_dtype=BF16, prologue="silu")
    m = _mm(low, ada_up, w_lead=(i,), out_dtype=F32) + ada_bias[i][None, :]
    return m.reshape(cond.shape[0], N_MOD, d)


def kernel(x, c, ctx, c_ctx, ada_down, ada_up, ada_bias, ln_g, ln_b, ffn_w_in, ffn_w_out, gla_w_in, gla_w_gdown, gla_w_gup, gla_b_g, gla_norm_g, gla_w_out, mlstm_w_in, mlstm_w_if, mlstm_b_if, mlstm_norm_g, mlstm_w_out, mla_w_down, mla_q_norm_g, mla_w_uq, mla_kv_norm_g, mla_w_ukv, mla_w_out):
    n_batch, seq, d = x.shape
    ctx_len = ctx.shape[1]
    depth = ada_down.shape[0]
    n_lat = n_batch * seq
    n_ctx = n_batch * ctx_len
    t_all = n_lat + n_ctx
    dims = (n_batch, seq, ctx_len, d)
    alpha = (2.0 * depth) ** 0.25
    n_seg = n_batch + 1
    assert seq % ELEMENTWISE_TILE_ROWS == 0 and n_ctx % math.gcd(n_ctx, ELEMENTWISE_TILE_ROWS) == 0

    def seg_of_row(row):
        return jnp.minimum(row // seq, n_batch)

    xs = jnp.concatenate([x.reshape(n_lat, d), ctx.reshape(n_ctx, d)], axis=0)
    cond = jnp.concatenate([c, c_ctx[None, :], jnp.zeros((8 - n_seg, d), F32)], axis=0)

    ffn_w_out_b, gla_w_out_b = ffn_w_out.astype(BF16), gla_w_out.astype(BF16)
    mlstm_w_out_b, mla_w_out_b = mlstm_w_out.astype(BF16), mla_w_out.astype(BF16)

    def mod_vecs(m, s, weight):
        shift = m[:n_seg, 3 * s][:, None, :]
        scale1p = 1.0 + m[:n_seg, 3 * s + 1][:, None, :]
        gate_w = weight * m[:n_seg, 3 * s + 2][:, None, :]
        return shift, scale1p, gate_w

    mods = [_modulations(cond, ada_down, ada_up, ada_bias, i) for i in range(depth)]
    shift0, scale0, _ = mod_vecs(mods[0], 0, MACARON_WEIGHT)
    hcur = _modulate(xs, scale0, shift0, seg_of_row=seg_of_row)

    def close(y_in, w, w_lead, xs, gate_w, g, b, scale1p, shift, rows):
        return _proj_ln(y_in, w, xs, gate_w, g, b, scale1p, shift, rows=rows, seg_of_row=seg_of_row,
                        seg_rows=math.gcd(seq, n_ctx), alpha=alpha, w_lead=w_lead)

    for i in range(depth):
        last = i == depth - 1
        m = mods[i]
        _, _, gw0 = mod_vecs(m, 0, MACARON_WEIGHT)
        sh1, sc1, gw1 = mod_vecs(m, 1, 1.0)
        sh2, sc2, gw2 = mod_vecs(m, 2, MACARON_WEIGHT)

        act = _ffn1(hcur, ffn_w_in, t_all, w_lead=(i, 0))
        xs, hcur = close(act, ffn_w_out_b,(i, 0), xs, gw0, ln_g[i, 0], ln_b[i, 0], sc1, sh1, t_all)

        kind, j = i % 3, i // 3
        rows1 = n_lat if last else t_all
        if kind == 0:
            y_in = _gla_mixer(hcur, gla_w_in, j, gla_w_gdown[j], gla_w_gup[j], gla_b_g[j], gla_norm_g[j], dims)
            w_o = gla_w_out_b
        elif kind == 1:
            y_in = _mlstm_mixer(hcur, mlstm_w_in, j, mlstm_w_if[j], mlstm_b_if[j], mlstm_norm_g[j], dims)
            w_o = mlstm_w_out_b
        else:
            y_in = _mla_mixer(hcur, mla_w_down[j], mla_q_norm_g[j], mla_w_uq[j], mla_kv_norm_g[j], mla_w_ukv[j],
                              dims, not last)
            w_o = mla_w_out_b
        xs, hcur = close(y_in, w_o, (j,), xs, gw1, ln_g[i, 1], ln_b[i, 1], sc2, sh2, rows1)

        if last:
            sh_n, sc_n = sh2, sc2
        else:
            sh_n, sc_n, _ = mod_vecs(mods[i + 1], 0, MACARON_WEIGHT)
        act = _ffn1(hcur, ffn_w_in, rows1, w_lead=(i, 1))
        xs, hcur = close(act, ffn_w_out_b,(i, 1), xs, gw2, ln_g[i, 2], ln_b[i, 2], sc_n, sh_n, rows1)

    return xs.reshape(n_batch, seq, d)
```
